```python
import math
import jax, jax.numpy as jnp
from jax import lax
import numpy as np

D_MODEL = 4096
BATCH = 4
SEQ = 2048
DEPTH = 1
DEC_BATCH = 128
DEC_SEQ = 4
PAST_LEN = 16384
PAGE_SIZE = 128

RET_HEADS = 8
RET_DK = 256
RET_DV = 256
RET_CHUNK = 128
ROPE_BASE = 10000.0
HG_HEADS = 16
HG_DK = 128
HG_DV = 128
HG_CHUNK = 16
D_FF = 11008
CONV_W = 3
EPS = 1e-6

RET_QK = RET_HEADS * RET_DK
RET_V = RET_HEADS * RET_DV
HG_K = HG_HEADS * HG_DK
HG_V = HG_HEADS * HG_DV
IN_COLS = 2 * RET_QK + 2 * RET_V + 2 * HG_K + 2 * HG_V + 2 * D_MODEL

kernel_name = "retnet_hgrn2_gated_parallel_convffn_step"


def rmsnorm(x, g):
    xf = x.astype(jnp.float32)
    y = xf * lax.rsqrt(jnp.mean(xf * xf, axis=-1, keepdims=True) + EPS)
    return (y * g.astype(jnp.float32)).astype(x.dtype)


def split_in(proj):
    widths = [RET_QK, RET_QK, RET_V, RET_V, HG_K, HG_K, HG_V, HG_V, D_MODEL, D_MODEL]
    offsets = np.cumsum(widths)[:-1].tolist()
    return jnp.split(proj, offsets, axis=-1)


def rotary(x, pos):
    half = x.shape[-1] // 2
    inv = ROPE_BASE ** (-jnp.arange(half, dtype=jnp.float32) / half)
    ang = pos[:, None] * inv[None, :]
    cos = jnp.cos(ang)[None, :, None, :]
    sin = jnp.sin(ang)[None, :, None, :]
    x1, x2 = x[..., :half], x[..., half:]
    return jnp.concatenate([x1 * cos - x2 * sin, x1 * sin + x2 * cos], axis=-1)


def to_chunks(a, L):
    B, T, H, d = a.shape
    return a.reshape(B, T // L, L, H, d).transpose(1, 0, 3, 2, 4)


def from_chunks(o):
    n, B, H, L, d = o.shape
    return o.transpose(1, 0, 3, 2, 4).reshape(B, n * L, H, d)


def retention_chunked(q, k, v, s0):
    T, H = q.shape[1], q.shape[2]
    L = math.gcd(T, RET_CHUNK)
    log_g = jnp.log1p(-jnp.exp2(-5.0 - jnp.arange(H, dtype=jnp.float32)))
    idx = jnp.arange(L, dtype=jnp.float32)
    diff = idx[:, None] - idx[None, :]
    causal = diff >= 0
    dmat = jnp.where(causal[None], jnp.exp(jnp.where(causal, diff, 0.0)[None] * log_g[:, None, None]), 0.0)
    q_dec = jnp.exp((idx + 1.0)[None, :] * log_g[:, None])[None, :, :, None]
    k_dec = jnp.exp((L - 1.0 - idx)[None, :] * log_g[:, None])[None, :, :, None]
    s_dec = jnp.exp(L * log_g)[None, :, None, None]
    f32 = jnp.float32
    qc, kc, vc = (to_chunks(a.astype(f32), L) for a in (q, k, v))

    def step(s, xs):
        qi, ki, vi = xs
        scores = jnp.einsum('bhid,bhjd->bhij', qi, ki) * dmat[None]
        o = jnp.einsum('bhij,bhjv->bhiv', scores, vi) + jnp.einsum('bhid,bhdv->bhiv', qi, s) * q_dec
        s = s * s_dec + jnp.einsum('bhjd,bhjv->bhdv', ki * k_dec, vi)
        return s, o

    s, o = lax.scan(step, s0.astype(f32), (qc, kc, vc))
    return from_chunks(o), s


def hgrn2_chunked(q, log_f, k, v, s0):
    T = q.shape[1]
    L = math.gcd(T, HG_CHUNK)
    mask = jnp.tril(jnp.ones((L, L), dtype=bool))[:, :, None]
    f32 = jnp.float32
    qc, lfc, kc, vc = (to_chunks(a.astype(f32), L) for a in (q, log_f, k, v))

    def step(s, xs):
        qi, lfi, ki, vi = xs
        b = jnp.cumsum(lfi, axis=2)
        rel = b[:, :, :, None, :] - b[:, :, None, :, :]
        decay = jnp.exp(jnp.where(mask, rel, -jnp.inf))
        scores = jnp.einsum('bhtk,bhtsk,bhsk->bhts', qi, decay, ki)
        o = jnp.einsum('bhts,bhsv->bhtv', scores, vi) + jnp.einsum('bhtk,bhkv->bhtv', qi * jnp.exp(b), s)
        b_last = b[:, :, -1]
        s = jnp.exp(b_last)[..., None] * s + jnp.einsum('bhsk,bhsv->bhkv', ki * jnp.exp(b_last[:, :, None] - b), vi)
        return s, o

    s, o = lax.scan(step, s0.astype(f32), (qc, lfc, kc, vc))
    return from_chunks(o), s


def head_rmsnorm(o, g, dtype):
    y = o * lax.rsqrt(jnp.mean(o * o, axis=-1, keepdims=True) + EPS)
    return (y * g.astype(jnp.float32)).astype(dtype)


def block(x, pos, s_ret, s_hg, conv_buf, g_mix, w_in, ret_g, hg_g, lb, w_br_ret, w_br_hg, w_out,
          g_ffn, w_gate, conv_w, conv_b, w_up, w_down):
    B, T, _ = x.shape
    dt = x.dtype
    xn = rmsnorm(x, g_mix)
    proj = xn @ w_in
    q_r, k_r, v_r, g_r, f_h, q_h, i_h, g_h, gate_ret, gate_hg = split_in(proj)
    qr = rotary(q_r.reshape(B, T, RET_HEADS, RET_DK).astype(jnp.float32), pos)
    kr = rotary(k_r.reshape(B, T, RET_HEADS, RET_DK).astype(jnp.float32), pos) * (RET_DK ** -0.5)
    vr = v_r.reshape(B, T, RET_HEADS, RET_DV)
    o_r, s_ret_new = retention_chunked(qr, kr, vr, s_ret)
    o_r = head_rmsnorm(o_r, ret_g.reshape(RET_HEADS, RET_DV), dt).reshape(B, T, RET_V) * jax.nn.silu(g_r)
    p_ret = o_r @ w_br_ret
    f = lb + (1.0 - lb) * jax.nn.sigmoid(f_h.astype(jnp.float32))
    log_f = jnp.log(f).reshape(B, T, HG_HEADS, HG_DK)
    kh = (1.0 - f).reshape(B, T, HG_HEADS, HG_DK)
    qh = q_h.reshape(B, T, HG_HEADS, HG_DK)
    vh = jax.nn.silu(i_h).reshape(B, T, HG_HEADS, HG_DV)
    o_h, s_hg_new = hgrn2_chunked(qh, log_f, kh, vh, s_hg)
    o_h = head_rmsnorm(o_h, hg_g.reshape(HG_HEADS, HG_DV), dt).reshape(B, T, HG_V) * jax.nn.silu(g_h)
    p_hg = o_h @ w_br_hg
    merged = jax.nn.sigmoid(gate_ret) * p_ret + jax.nn.sigmoid(gate_hg) * p_hg
    x = x + merged @ w_out
    xn = rmsnorm(x, g_ffn)
    u = xn @ w_gate
    full = jnp.concatenate([conv_buf.astype(u.dtype), u], axis=1)
    c = conv_b + sum(full[:, j:j + T] * conv_w[j] for j in range(CONV_W))
    h = jax.nn.silu(c) * (xn @ w_up)
    x = x + h @ w_down
    conv_new = full[:, T:]
    return x, s_ret_new.astype(dt), s_hg_new.astype(dt), conv_new


def setup_inputs(seed: int = 0) -> dict:
    key = jax.random.key(seed)
    ks = jax.random.split(key, 24)
    f32 = jnp.float32
    nrm = lambda k, shape, s: jax.random.normal(k, shape, f32) * s
    return {
        "x_prompt": nrm(ks[0], (BATCH, SEQ, D_MODEL), 1.0),
        "x_sample": nrm(ks[1], (DEC_BATCH, DEC_SEQ, D_MODEL), 1.0),
        "state_ret": nrm(ks[2], (DEPTH, DEC_BATCH, RET_HEADS, RET_DK, RET_DV), 0.5),
        "state_hgrn": nrm(ks[3], (DEPTH, DEC_BATCH, HG_HEADS, HG_DK, HG_DV), 0.5),
        "state_ffn_conv": nrm(ks[4], (DEPTH, DEC_BATCH, CONV_W - 1, D_FF), 1.0),
        "norm_mix_g": 1.0 + nrm(ks[5], (DEPTH, D_MODEL), 0.02),
        "w_in": nrm(ks[6], (DEPTH, D_MODEL, IN_COLS), D_MODEL ** -0.5),
        "ret_norm_g": 1.0 + nrm(ks[7], (DEPTH, RET_V), 0.02),
        "hg_norm_g": 1.0 + nrm(ks[8], (DEPTH, HG_V), 0.02),
        "hg_lb_logits": nrm(ks[9], (DEPTH + 1, HG_K), 0.5),
        "w_br_ret": nrm(ks[10], (DEPTH, RET_V, D_MODEL), RET_V ** -0.5),
        "w_br_hg": nrm(ks[11], (DEPTH, HG_V, D_MODEL), HG_V ** -0.5),
        "w_out": nrm(ks[12], (DEPTH, D_MODEL, D_MODEL), D_MODEL ** -0.5),
        "norm_ffn_g": 1.0 + nrm(ks[13], (DEPTH, D_MODEL), 0.02),
        "w_gate": nrm(ks[14], (DEPTH, D_MODEL, D_FF), D_MODEL ** -0.5),
        "conv_w": nrm(ks[15], (DEPTH, CONV_W, D_FF), CONV_W ** -0.5),
        "conv_b": nrm(ks[16], (DEPTH, D_FF), 0.02),
        "w_up": nrm(ks[17], (DEPTH, D_MODEL, D_FF), D_MODEL ** -0.5),
        "w_down": nrm(ks[18], (DEPTH, D_FF, D_MODEL), D_FF ** -0.5),
        "final_norm_g": 1.0 + nrm(ks[19], (D_MODEL,), 0.02),
    }


def reference(x_prompt, x_sample, state_ret, state_hgrn, state_ffn_conv, norm_mix_g, w_in, ret_norm_g,
              hg_norm_g, hg_lb_logits, w_br_ret, w_br_hg, w_out, norm_ffn_g, w_gate, conv_w, conv_b,
              w_up, w_down, final_norm_g):
    f32 = jnp.float32
    Bp, Tp, _ = x_prompt.shape
    Ts = x_sample.shape[1]
    pos_p = jnp.arange(Tp, dtype=f32)
    pos_s = PAST_LEN + jnp.arange(Ts, dtype=f32)
    lb_all = jnp.cumsum(jax.nn.softmax(hg_lb_logits.astype(f32), axis=0), axis=0)
    yp, ys = x_prompt, x_sample
    rp, hp, cp, rs, hs, cs = [], [], [], [], [], []
    for l in range(DEPTH):
        params = (norm_mix_g[l], w_in[l], ret_norm_g[l], hg_norm_g[l], lb_all[l], w_br_ret[l], w_br_hg[l],
                  w_out[l], norm_ffn_g[l], w_gate[l], conv_w[l], conv_b[l], w_up[l], w_down[l])
        zr = jnp.zeros((Bp, RET_HEADS, RET_DK, RET_DV), f32)
        zh = jnp.zeros((Bp, HG_HEADS, HG_DK, HG_DV), f32)
        zc = jnp.zeros((Bp, CONV_W - 1, D_FF), x_prompt.dtype)
        yp, r1, h1, c1 = block(yp, pos_p, zr, zh, zc, *params)
        ys, r2, h2, c2 = block(ys, pos_s, state_ret[l], state_hgrn[l], state_ffn_conv[l], *params)
        rp.append(r1); hp.append(h1); cp.append(c1)
        rs.append(r2); hs.append(h2); cs.append(c2)
    y_prompt = rmsnorm(yp, final_norm_g)
    y_sample = rmsnorm(ys, final_norm_g)
    return (y_prompt, y_sample, jnp.stack(rp), jnp.stack(hp), jnp.stack(cp),
            jnp.stack(rs), jnp.stack(hs), jnp.stack(cs))
```

```python
import functools
import math

import jax
import jax.numpy as jnp
from jax import lax
from jax.experimental import pallas as pl
from jax.experimental.pallas import tpu as pltpu

F32 = jnp.float32
BF16 = jnp.bfloat16

D_MODEL = 4096
PAST_LEN = 16384
RET_HEADS = 8
RET_DK = 256
RET_DV = 256
RET_CHUNK = 128
ROPE_BASE = 10000.0
HG_HEADS = 16
HG_DK = 128
HG_DV = 128
HG_CHUNK = 16
D_FF = 11008
CONV_W = 3
EPS = 1e-6

RET_QK = RET_HEADS * RET_DK
RET_V = RET_HEADS * RET_DV
HG_K = HG_HEADS * HG_DK
HG_V = HG_HEADS * HG_DV
COL_QR = 0
COL_KR = COL_QR + RET_QK
COL_VR = COL_KR + RET_QK
COL_GR = COL_VR + RET_V
COL_FH = COL_GR + RET_V
COL_QH = COL_FH + HG_K
COL_IH = COL_QH + HG_K
COL_GH = COL_IH + HG_V
COL_GATE_RET = COL_GH + HG_V
COL_GATE_HG = COL_GATE_RET + D_MODEL
IN_COLS = COL_GATE_HG + D_MODEL

LANES = 128
SUBLANES = 8
V7X_VMEM_BYTES = 64 * 1024 * 1024
MIB = 1024 * 1024

_NT = (((1,), (1,)), ((), ()))
_TN = (((0,), (0,)), ((), ()))


def _params(semantics, vmem_bytes):
    return pltpu.CompilerParams(dimension_semantics=semantics,
                                vmem_limit_bytes=min(int(vmem_bytes), V7X_VMEM_BYTES - 4 * MIB))


def _silu(x):
    return x * jax.nn.sigmoid(x)


def _rms_kernel(x_ref, g_ref, o_ref):
    x = x_ref[...]
    y = x * lax.rsqrt(jnp.mean(x * x, axis=-1, keepdims=True) + EPS)
    o_ref[...] = (y * g_ref[...]).astype(o_ref.dtype)


def _rms_add_kernel(x_ref, d_ref, g_ref, o_ref):
    x = x_ref[...] + d_ref[...]
    y = x * lax.rsqrt(jnp.mean(x * x, axis=-1, keepdims=True) + EPS)
    o_ref[...] = (y * g_ref[...]).astype(o_ref.dtype)


def _rmsnorm(x, g, out_dtype, add=None, tr=256):
    m, d = x.shape
    row = pl.BlockSpec((tr, d), lambda i: (i, 0))
    vec = pl.BlockSpec((1, d), lambda i: (0, 0))
    ins = [x] if add is None else [x, add]
    return pl.pallas_call(
        _rms_kernel if add is None else _rms_add_kernel,
        out_shape=jax.ShapeDtypeStruct((m, d), out_dtype),
        grid=(m // tr,),
        in_specs=[row] * len(ins) + [vec],
        out_specs=row,
        compiler_params=_params(("parallel",), 2 * (len(ins) + 2) * tr * d * 4),
        name="rmsnorm",
    )(*ins, g.reshape(1, d))


def _mm_kernel(x_ref, w_ref, o_ref):
    o_ref[...] = jnp.dot(x_ref[...], w_ref[...].astype(BF16), preferred_element_type=F32)


def _in_proj(xn, w, tm, tn=512):
    m, k = xn.shape
    n = w.shape[1]
    vmem = 2 * (tm * k * 2 + k * tn * 4 + tm * tn * 4) + k * tn * 2 + tm * tn * 4
    return pl.pallas_call(
        _mm_kernel,
        out_shape=jax.ShapeDtypeStruct((m, n), F32),
        grid=(m // tm, n // tn),
        in_specs=[pl.BlockSpec((tm, k), lambda i, j: (i, 0)),
                  pl.BlockSpec((k, tn), lambda i, j: (0, j))],
        out_specs=pl.BlockSpec((tm, tn), lambda i, j: (i, j)),
        compiler_params=_params(("parallel", "arbitrary"), vmem + 4 * MIB),
        name="in_proj",
    )(xn, w)


def _merge_kernel(or_ref, oh_ref, wr_ref, wh_ref, gr_ref, gh_ref, o_ref):
    pr = jnp.dot(or_ref[...], wr_ref[...].astype(BF16), preferred_element_type=F32)
    ph = jnp.dot(oh_ref[...], wh_ref[...].astype(BF16), preferred_element_type=F32)
    o_ref[...] = (jax.nn.sigmoid(gr_ref[...]) * pr + jax.nn.sigmoid(gh_ref[...]) * ph).astype(o_ref.dtype)


def _merge(o_r, o_h, w_br_ret, w_br_hg, proj, tm, tn=512):
    m = o_r.shape[0]
    kr, kh = o_r.shape[1], o_h.shape[1]
    n = D_MODEL
    vmem = 2 * (tm * (kr + kh) * 2 + (kr + kh) * tn * 4 + 2 * tm * tn * 4 + tm * tn * 2)
    vmem += (kr + kh) * tn * 2 + 3 * tm * tn * 4
    return pl.pallas_call(
        _merge_kernel,
        out_shape=jax.ShapeDtypeStruct((m, n), BF16),
        grid=(m // tm, n // tn),
        in_specs=[pl.BlockSpec((tm, kr), lambda i, j: (i, 0)),
                  pl.BlockSpec((tm, kh), lambda i, j: (i, 0)),
                  pl.BlockSpec((kr, tn), lambda i, j: (0, j)),
                  pl.BlockSpec((kh, tn), lambda i, j: (0, j)),
                  pl.BlockSpec((tm, tn), lambda i, j: (i, COL_GATE_RET // tn + j)),
                  pl.BlockSpec((tm, tn), lambda i, j: (i, COL_GATE_HG // tn + j))],
        out_specs=pl.BlockSpec((tm, tn), lambda i, j: (i, j)),
        compiler_params=_params(("parallel", "arbitrary"), vmem + 4 * MIB),
        name="merge",
    )(o_r, o_h, w_br_ret, w_br_hg, proj, proj)


def _out_proj_kernel(m_ref, w_ref, x_ref, o_ref):
    o_ref[...] = x_ref[...] + jnp.dot(m_ref[...], w_ref[...].astype(BF16), preferred_element_type=F32)


def _out_proj(merged, w, x, tm, tn=512):
    m, k = merged.shape
    n = w.shape[1]
    vmem = 2 * (tm * k * 2 + k * tn * 4 + 2 * tm * tn * 4) + k * tn * 2 + tm * tn * 4
    return pl.pallas_call(
        _out_proj_kernel,
        out_shape=jax.ShapeDtypeStruct((m, n), F32),
        grid=(m // tm, n // tn),
        in_specs=[pl.BlockSpec((tm, k), lambda i, j: (i, 0)),
                  pl.BlockSpec((k, tn), lambda i, j: (0, j)),
                  pl.BlockSpec((tm, tn), lambda i, j: (i, j))],
        out_specs=pl.BlockSpec((tm, tn), lambda i, j: (i, j)),
        compiler_params=_params(("parallel", "arbitrary"), vmem + 4 * MIB),
        name="out_proj",
    )(merged, w, x)


def _ffn_down_kernel(h_ref, w_ref, o_ref):
    @pl.when(pl.program_id(1) == 0)
    def _():
        o_ref[...] = jnp.zeros_like(o_ref)

    o_ref[...] += jnp.dot(h_ref[...], w_ref[...].astype(BF16), preferred_element_type=F32)


def _ffn_down(h, w, tm, tk=256):
    m, f = h.shape
    n = w.shape[1]
    vmem = 2 * (tm * tk * 2 + tk * n * 4 + tm * n * 4) + tk * n * 2 + tm * n * 4
    return pl.pallas_call(
        _ffn_down_kernel,
        out_shape=jax.ShapeDtypeStruct((m, n), F32),
        grid=(m // tm, f // tk),
        in_specs=[pl.BlockSpec((tm, tk), lambda i, k: (i, k)),
                  pl.BlockSpec((tk, n), lambda i, k: (k, 0))],
        out_specs=pl.BlockSpec((tm, n), lambda i, k: (i, 0)),
        compiler_params=_params(("parallel", "arbitrary"), vmem + 4 * MIB),
        name="ffn_down",
    )(h, w)


def _conv_gate(full0, full1, full2, cw_ref, cb_ref, up):
    c = cb_ref[...] + ((full0 * cw_ref[0:1, :] + full1 * cw_ref[1:2, :]) + full2 * cw_ref[2:3, :])
    return _silu(c) * up


def _ffn_up_prompt_kernel(xn_ref, wg_ref, wu_ref, cw_ref, cb_ref, h_ref, tail_ref,
                          wg_s, wu_s, u_s, *, tm, blocks_per_seq):
    i = pl.program_id(1)

    @pl.when(i == 0)
    def _():
        wg_s[...] = wg_ref[...].astype(BF16)
        wu_s[...] = wu_ref[...].astype(BF16)

    @pl.when(i % blocks_per_seq == 0)
    def _():
        u_s[0:SUBLANES, :] = jnp.zeros((SUBLANES, u_s.shape[1]), F32)

    xn = xn_ref[...]
    u = jnp.dot(xn, wg_s[...], preferred_element_type=F32)
    up = jnp.dot(xn, wu_s[...], preferred_element_type=F32)
    u_s[SUBLANES:SUBLANES + tm, :] = u
    h = _conv_gate(u_s[SUBLANES - 2:SUBLANES - 2 + tm, :], u_s[SUBLANES - 1:SUBLANES - 1 + tm, :], u,
                   cw_ref, cb_ref, up)
    h_ref[...] = h.astype(h_ref.dtype)
    last = u[tm - SUBLANES:tm, :]
    tail_ref[0] = last
    u_s[0:SUBLANES, :] = last


def _ffn_up_prompt(xn, w_gate, w_up, conv_w, conv_b, n_seq, tm=1024, tf=256):
    m, d = xn.shape
    f = w_gate.shape[1]
    blocks_per_seq = m // n_seq // tm
    vmem = 2 * (tm * d * 2 + 2 * d * tf * 4 + tm * tf * 2) + 2 * d * tf * 2 + (tm + SUBLANES) * tf * 4
    vmem += 4 * tm * tf * 4
    return pl.pallas_call(
        functools.partial(_ffn_up_prompt_kernel, tm=tm, blocks_per_seq=blocks_per_seq),
        out_shape=(jax.ShapeDtypeStruct((m, f), BF16),
                   jax.ShapeDtypeStruct((n_seq, SUBLANES, f), F32)),
        grid=(f // tf, m // tm),
        in_specs=[pl.BlockSpec((tm, d), lambda j, i: (i, 0)),
                  pl.BlockSpec((d, tf), lambda j, i: (0, j)),
                  pl.BlockSpec((d, tf), lambda j, i: (0, j)),
                  pl.BlockSpec((CONV_W, tf), lambda j, i: (0, j)),
                  pl.BlockSpec((1, tf), lambda j, i: (0, j))],
        out_specs=(pl.BlockSpec((tm, tf), lambda j, i: (i, j)),
                   pl.BlockSpec((1, SUBLANES, tf), lambda j, i: (i // blocks_per_seq, 0, j))),
        scratch_shapes=[pltpu.VMEM((d, tf), BF16), pltpu.VMEM((d, tf), BF16),
                        pltpu.VMEM((tm + SUBLANES, tf), F32)],
        compiler_params=_params(("arbitrary", "arbitrary"), vmem + 4 * MIB),
        name="ffn_up_prompt",
    )(xn, w_gate, w_up, conv_w, conv_b.reshape(1, f))


def _ffn_up_sample_kernel(xn_ref, wg_ref, wu_ref, cw_ref, cb_ref, b0_ref, b1_ref,
                          h_ref, t0_ref, t1_ref, u_s, up_s, h_s, *, n_seq, t_len):
    xn = xn_ref[...]
    u = jnp.dot(xn, wg_ref[...].astype(BF16), preferred_element_type=F32)
    up = jnp.dot(xn, wu_ref[...].astype(BF16), preferred_element_type=F32)
    for l in range(u_s.shape[0]):
        ls = slice(l * LANES, (l + 1) * LANES)
        u_s[l] = u[:, ls]
        up_s[l] = up[:, ls]
        full = [b0_ref[:, ls], b1_ref[:, ls]]
        full += [u_s[l, pl.ds(t, n_seq, stride=t_len), :] for t in range(t_len)]
        for t in range(t_len):
            c = cb_ref[:, ls] + ((full[t] * cw_ref[0:1, ls] + full[t + 1] * cw_ref[1:2, ls])
                                 + full[t + 2] * cw_ref[2:3, ls])
            h_s[l, pl.ds(t, n_seq, stride=t_len), :] = _silu(c) * up_s[l, pl.ds(t, n_seq, stride=t_len), :]
        h_ref[:, ls] = h_s[l].astype(h_ref.dtype)
        t0_ref[:, ls] = full[t_len]
        t1_ref[:, ls] = full[t_len + 1]


def _ffn_up_sample(xn, w_gate, w_up, conv_w, conv_b, conv_buf, n_seq, tf=256):
    m, d = xn.shape
    f = w_gate.shape[1]
    t_len = m // n_seq
    buf2d = conv_buf.reshape(n_seq, (CONV_W - 1) * f)
    vmem = 2 * (m * d * 2 + 2 * d * tf * 4 + m * tf * 2 + 4 * n_seq * tf * 4) + 2 * d * tf * 2 + 5 * m * tf * 4
    return pl.pallas_call(
        functools.partial(_ffn_up_sample_kernel, n_seq=n_seq, t_len=t_len),
        out_shape=(jax.ShapeDtypeStruct((m, f), BF16),
                   jax.ShapeDtypeStruct((n_seq, f), F32),
                   jax.ShapeDtypeStruct((n_seq, f), F32)),
        grid=(f // tf,),
        in_specs=[pl.BlockSpec((m, d), lambda j: (0, 0)),
                  pl.BlockSpec((d, tf), lambda j: (0, j)),
                  pl.BlockSpec((d, tf), lambda j: (0, j)),
                  pl.BlockSpec((CONV_W, tf), lambda j: (0, j)),
                  pl.BlockSpec((1, tf), lambda j: (0, j)),
                  pl.BlockSpec((n_seq, tf), lambda j: (0, j)),
                  pl.BlockSpec((n_seq, tf), lambda j: (0, f // tf + j))],
        out_specs=(pl.BlockSpec((m, tf), lambda j: (0, j)),
                   pl.BlockSpec((n_seq, tf), lambda j: (0, j)),
                   pl.BlockSpec((n_seq, tf), lambda j: (0, j))),
        scratch_shapes=[pltpu.VMEM((tf // LANES, m, LANES), F32)] * 3,
        compiler_params=_params(("parallel",), vmem + 4 * MIB),
        name="ffn_up_sample",
    )(xn, w_gate, w_up, conv_w, conv_b.reshape(1, f), buf2d, buf2d)


def _retention_kernel(lg_ref, q_ref, k_ref, v_ref, g_ref, cos_ref, sin_ref, gn_ref, *rest,
                      chunk, n_seg, carry):
    if carry:
        o_ref, sout_ref = rest
    else:
        s0_ref, o_ref, sout_ref = rest
    rows = chunk * n_seg
    shift = int(math.log2(chunk))
    log_g = lg_ref[pl.program_id(1)]
    half = RET_DK // 2
    cos = cos_ref[...]
    sin = sin_ref[...]

    def rot(x):
        x1 = x[:, :half]
        x2 = x[:, half:]
        return jnp.concatenate([x1 * cos - x2 * sin, x1 * sin + x2 * cos], axis=-1)

    qr = rot(q_ref[...])
    kr = rot(k_ref[...]) * (RET_DK ** -0.5)
    ri = lax.broadcasted_iota(jnp.int32, (rows, rows), 0)
    ci = lax.broadcasted_iota(jnp.int32, (rows, rows), 1)
    ok = ri >= ci
    if n_seg > 1:
        ok = jnp.logical_and(ok, lax.shift_right_logical(ri, shift) == lax.shift_right_logical(ci, shift))
    diff = (ri - ci).astype(F32)
    dmat = jnp.where(ok, jnp.exp(jnp.where(ok, diff, 0.0) * log_g), 0.0)
    row_id = lax.broadcasted_iota(jnp.int32, (rows, RET_DK), 0)
    pos = (row_id & (chunk - 1)).astype(F32)
    q_dec = jnp.exp((pos + 1.0) * log_g)
    k_dec = jnp.exp((chunk - 1.0 - pos) * log_g)
    s_dec = jnp.exp(jnp.full((1, RET_DV), float(chunk), F32) * log_g)

    qb = qr.astype(BF16)
    kb = kr.astype(BF16)
    vb = v_ref[...].astype(BF16)
    scores = lax.dot_general(qb, kb, _NT, preferred_element_type=F32) * dmat
    o = jnp.dot(scores.astype(BF16), vb, preferred_element_type=F32)
    kd = kr * k_dec
    if carry:
        @pl.when(pl.program_id(2) == 0)
        def _():
            sout_ref[0, 0] = jnp.zeros((RET_DK, RET_DV), F32)

        s = sout_ref[0, 0]
        o = o + jnp.dot(qb, s.astype(BF16), preferred_element_type=F32) * q_dec
        sout_ref[0, 0] = s * s_dec + lax.dot_general(kd.astype(BF16), vb, _TN, preferred_element_type=F32)
    else:
        seg = lax.shift_right_logical(row_id, shift)
        for b in range(n_seg):
            mine = seg == b
            s = s0_ref[b, 0]
            o_b = jnp.dot(qb, s.astype(BF16), preferred_element_type=F32) * q_dec
            o = o + jnp.where(mine, o_b, 0.0)
            kd_b = jnp.where(mine, kd, 0.0).astype(BF16)
            sout_ref[b, 0] = s * s_dec + lax.dot_general(kd_b, vb, _TN, preferred_element_type=F32)

    y = o * lax.rsqrt(jnp.mean(o * o, axis=-1, keepdims=True) + EPS) * gn_ref[...]
    o_ref[...] = (y * _silu(g_ref[...])).astype(o_ref.dtype)


def _retention(proj, log_g, cos, sin, ret_norm_g, state, n_seq, chunk, n_seg):
    m = proj.shape[0]
    rows = chunk * n_seg
    carry = state is None
    cq, ck, cv, cg = (c // RET_DK for c in (COL_QR, COL_KR, COL_VR, COL_GR))
    if carry:
        n_chunks = m // n_seq // rows
        grid = (n_seq, RET_HEADS, n_chunks)
        row_blk = lambda b, h, c, lg: b * n_chunks + c
        tab_blk = lambda b, h, c, lg: (c, 0)
        st_blk = lambda b, h, c, lg: (b, h, 0, 0)
        st_shape = (1, 1, RET_DK, RET_DV)
        sem = ("parallel", "parallel", "arbitrary")
    else:
        grid = (m // rows, RET_HEADS)
        row_blk = lambda i, h, lg: i
        tab_blk = lambda i, h, lg: (0, 0)
        st_blk = lambda i, h, lg: (i, h, 0, 0)
        st_shape = (n_seg, 1, RET_DK, RET_DV)
        sem = ("parallel", "parallel")

    def col(c0):
        return pl.BlockSpec((rows, RET_DK), lambda *a: (row_blk(*a), c0 + a[1]))

    in_specs = [col(cq), col(ck), col(cv), col(cg),
                pl.BlockSpec((rows, RET_DK // 2), tab_blk),
                pl.BlockSpec((rows, RET_DK // 2), tab_blk),
                pl.BlockSpec((1, RET_DV), lambda *a: (0, a[1]))]
    args = [proj, proj, proj, proj, cos, sin, ret_norm_g.reshape(1, RET_V)]
    if not carry:
        in_specs.append(pl.BlockSpec(st_shape, st_blk))
        args.append(state)
    st_bytes = n_seg * RET_DK * RET_DV * 4
    vmem = 2 * (6 * rows * RET_DK * 4 + 2 * st_bytes) + 16 * rows * RET_DK * 4 + 4 * RET_DK * RET_DV * 4
    return pl.pallas_call(
        functools.partial(_retention_kernel, chunk=chunk, n_seg=n_seg, carry=carry),
        out_shape=(jax.ShapeDtypeStruct((m, RET_V), BF16),
                   jax.ShapeDtypeStruct((n_seq, RET_HEADS, RET_DK, RET_DV), F32)),
        grid_spec=pltpu.PrefetchScalarGridSpec(
            num_scalar_prefetch=1, grid=grid, in_specs=in_specs,
            out_specs=(pl.BlockSpec((rows, RET_DV), lambda *a: (row_blk(*a), a[1])),
                       pl.BlockSpec(st_shape, st_blk))),
        compiler_params=_params(sem, vmem + 8 * MIB),
        name="retention_prompt" if carry else "retention_sample",
    )(log_g, *args)


def _hg_gates(f_raw, i_raw, lb):
    f = lb + (1.0 - lb) * jax.nn.sigmoid(f_raw)
    return jnp.log(f), 1.0 - f, _silu(i_raw)


def _hg_lower_bound(lb_ref):
    logits = lb_ref[...]
    e = jnp.exp(logits - jnp.max(logits, axis=0, keepdims=True))
    return e[0:1, :] / jnp.sum(e, axis=0, keepdims=True)


def _hg_cumsum(log_f, pos, chunk):
    b = log_f
    sh = 1
    while sh < chunk:
        b = b + jnp.where(pos >= sh, pltpu.roll(b, sh, 0), 0.0)
        sh *= 2
    return b


def _hg_intra(q, k, v, b, pos, chunk):
    o = jnp.sum(q * k, axis=-1, keepdims=True) * v
    for d in range(1, chunk):
        valid = pos >= d
        decay = jnp.exp(jnp.where(valid, b - pltpu.roll(b, d, 0), -jnp.inf))
        score = jnp.sum(q * decay * pltpu.roll(k, d, 0), axis=-1, keepdims=True)
        o = o + score * pltpu.roll(v, d, 0)
    return o


def _hg_finish(o, g_raw, gn):
    y = o * lax.rsqrt(jnp.mean(o * o, axis=-1, keepdims=True) + EPS) * gn
    return y * _silu(g_raw)


def _hgrn_prompt_kernel(f_ref, q_ref, i_ref, g_ref, lb_ref, gn_ref, o_ref, sout_ref, st_ref,
                        *, heads, rows, chunk):
    tb = pl.program_id(2)

    @pl.when(tb == 0)
    def _():
        st_ref[...] = jnp.zeros_like(st_ref)

    lb = _hg_lower_bound(lb_ref)
    gn = gn_ref[...]
    pos = lax.broadcasted_iota(jnp.int32, (chunk, HG_DK), 0)

    def body(c, carry):
        r0 = pl.multiple_of(c * chunk, chunk)
        rs = pl.ds(r0, chunk)
        for h in range(heads):
            cs = slice(h * HG_DK, (h + 1) * HG_DK)
            q = q_ref[rs, cs]
            log_f, k, v = _hg_gates(f_ref[rs, cs], i_ref[rs, cs], lb[:, cs])
            b = _hg_cumsum(log_f, pos, chunk)
            o = _hg_intra(q, k, v, b, pos, chunk)
            st = st_ref[h]
            o = o + lax.dot_general((q * jnp.exp(b)).astype(BF16), st.astype(BF16), _NT,
                                    preferred_element_type=F32)
            b_last = b[chunk - 1:chunk, :]
            k_dec = (k * jnp.exp(b_last - b)).astype(BF16)
            st_ref[h] = st * jnp.exp(b_last) + lax.dot_general(v.astype(BF16), k_dec, _TN,
                                                               preferred_element_type=F32)
            o_ref[rs, cs] = _hg_finish(o, g_ref[rs, cs], gn[:, cs]).astype(o_ref.dtype)
        return carry

    lax.fori_loop(0, rows // chunk, body, 0)

    @pl.when(tb == pl.num_programs(2) - 1)
    def _():
        for h in range(heads):
            sout_ref[0, h] = st_ref[h].T


def _hgrn_prompt(proj, lb_logits, hg_norm_g, n_seq, chunk, heads=8, rows=512):
    m = proj.shape[0]
    n_tb = m // n_seq // rows
    width = heads * HG_DK
    cf, cq, ci, cg = (c // width for c in (COL_FH, COL_QH, COL_IH, COL_GH))

    def col(c0):
        return pl.BlockSpec((rows, width), lambda b, hg, t: (b * n_tb + t, c0 + hg))

    vmem = 2 * (4 * rows * width * 4 + rows * width * 2 + heads * HG_DK * HG_DV * 4) + heads * HG_DK * HG_DV * 4
    return pl.pallas_call(
        functools.partial(_hgrn_prompt_kernel, heads=heads, rows=rows, chunk=chunk),
        out_shape=(jax.ShapeDtypeStruct((m, HG_V), BF16),
                   jax.ShapeDtypeStruct((n_seq, HG_HEADS, HG_DK, HG_DV), F32)),
        grid=(n_seq, HG_HEADS // heads, n_tb),
        in_specs=[col(cf), col(cq), col(ci), col(cg),
                  pl.BlockSpec((lb_logits.shape[0], width), lambda b, hg, t: (0, hg)),
                  pl.BlockSpec((1, width), lambda b, hg, t: (0, hg))],
        out_specs=(pl.BlockSpec((rows, width), lambda b, hg, t: (b * n_tb + t, hg)),
                   pl.BlockSpec((1, heads, HG_DK, HG_DV), lambda b, hg, t: (b, hg, 0, 0))),
        scratch_shapes=[pltpu.VMEM((heads, HG_DV, HG_DK), F32)],
        compiler_params=_params(("parallel", "parallel", "arbitrary"), vmem + 8 * MIB),
        name="hgrn_prompt",
    )(proj, proj, proj, proj, lb_logits, hg_norm_g.reshape(1, HG_V))


def _hgrn_sample_kernel(f_ref, q_ref, i_ref, g_ref, lb_ref, gn_ref, s0_ref, o_ref, sout_ref,
                        *, heads, n_seg, chunk):
    rows = n_seg * chunk
    shift = int(math.log2(chunk))
    lb = _hg_lower_bound(lb_ref)
    gn = gn_ref[...]
    row_id = lax.broadcasted_iota(jnp.int32, (rows, HG_DK), 0)
    pos = row_id & (chunk - 1)
    seg = lax.shift_right_logical(row_id, shift)
    eye = (lax.broadcasted_iota(jnp.int32, (HG_DK, HG_DK), 0)
           == lax.broadcasted_iota(jnp.int32, (HG_DK, HG_DK), 1))
    for h in range(heads):
        cs = slice(h * HG_DK, (h + 1) * HG_DK)
        q = q_ref[:, cs]
        log_f, k, v = _hg_gates(f_ref[:, cs], i_ref[:, cs], lb[:, cs])
        b = _hg_cumsum(log_f, pos, chunk)
        o = _hg_intra(q, k, v, b, pos, chunk)
        b_last = b
        for j in range(1, chunk):
            b_last = jnp.where(pos == chunk - 1 - j, pltpu.roll(b, rows - j, 0), b_last)
        q_dec = (q * jnp.exp(b)).astype(BF16)
        k_dec = k * jnp.exp(b_last - b)
        vb = v.astype(BF16)
        for s in range(n_seg):
            mine = seg == s
            st = s0_ref[s, h]
            o = o + jnp.where(mine, jnp.dot(q_dec, st.astype(BF16), preferred_element_type=F32), 0.0)
            e_row = jnp.exp(b[(s + 1) * chunk - 1:(s + 1) * chunk, :])
            e_col = jnp.sum(jnp.where(eye, e_row, 0.0), axis=-1, keepdims=True)
            k_s = jnp.where(mine, k_dec, 0.0).astype(BF16)
            sout_ref[s, h] = e_col * st + lax.dot_general(k_s, vb, _TN, preferred_element_type=F32)
        o_ref[:, cs] = _hg_finish(o, g_ref[:, cs], gn[:, cs]).astype(o_ref.dtype)


def _hgrn_sample(proj, lb_logits, hg_norm_g, state, chunk, heads=8, n_seg=8):
    m = proj.shape[0]
    n_seq = m // chunk
    rows = n_seg * chunk
    width = heads * HG_DK
    cf, cq, ci, cg = (c // width for c in (COL_FH, COL_QH, COL_IH, COL_GH))

    def col(c0):
        return pl.BlockSpec((rows, width), lambda i, hg: (i, c0 + hg))

    st_spec = pl.BlockSpec((n_seg, heads, HG_DK, HG_DV), lambda i, hg: (i, hg, 0, 0))
    st_bytes = n_seg * heads * HG_DK * HG_DV * 4
    vmem = 2 * (5 * rows * width * 4 + 2 * st_bytes)
    return pl.pallas_call(
        functools.partial(_hgrn_sample_kernel, heads=heads, n_seg=n_seg, chunk=chunk),
        out_shape=(jax.ShapeDtypeStruct((m, HG_V), BF16),
                   jax.ShapeDtypeStruct((n_seq, HG_HEADS, HG_DK, HG_DV), F32)),
        grid=(m // rows, HG_HEADS // heads),
        in_specs=[col(cf), col(cq), col(ci), col(cg),
                  pl.BlockSpec((lb_logits.shape[0], width), lambda i, hg: (0, hg)),
                  pl.BlockSpec((1, width), lambda i, hg: (0, hg)),
                  st_spec],
        out_specs=(pl.BlockSpec((rows, width), lambda i, hg: (i, hg)), st_spec),
        compiler_params=_params(("parallel", "parallel"), vmem + 8 * MIB),
        name="hgrn_sample",
    )(proj, proj, proj, proj, lb_logits, hg_norm_g.reshape(1, HG_V), state)


def _rope_tables(pos):
    half = RET_DK // 2
    inv = ROPE_BASE ** (-jnp.arange(half, dtype=F32) / half)
    ang = pos[:, None] * inv[None, :]
    return jnp.cos(ang), jnp.sin(ang)


def _trunk(x, n_seq, pos, ret_state, hg_state, conv_buf, p, ret_seg, hg_seg):
    m = x.shape[0]
    t_len = m // n_seq
    prompt = ret_state is None
    tm = min(m, 1024)
    ret_chunk = math.gcd(t_len, RET_CHUNK)
    hg_chunk = math.gcd(t_len, HG_CHUNK)
    log_g = jnp.log1p(-jnp.exp2(-5.0 - jnp.arange(RET_HEADS, dtype=F32)))
    cos, sin = _rope_tables(pos)
    if not prompt:
        cos = jnp.tile(cos, (ret_seg, 1))
        sin = jnp.tile(sin, (ret_seg, 1))

    xn = _rmsnorm(x, p["norm_mix_g"], BF16)
    proj = _in_proj(xn, p["w_in"], tm)
    o_r, ret_new = _retention(proj, log_g, cos, sin, p["ret_norm_g"], ret_state, n_seq, ret_chunk,
                              1 if prompt else ret_seg)
    if prompt:
        o_h, hg_new = _hgrn_prompt(proj, p["hg_lb_logits"], p["hg_norm_g"], n_seq, hg_chunk)
    else:
        o_h, hg_new = _hgrn_sample(proj, p["hg_lb_logits"], p["hg_norm_g"], hg_state, hg_chunk, n_seg=hg_seg)
    merged = _merge(o_r, o_h, p["w_br_ret"], p["w_br_hg"], proj, tm)
    x1 = _out_proj(merged, p["w_out"], x, tm)

    xn2 = _rmsnorm(x1, p["norm_ffn_g"], BF16)
    if prompt:
        h, tail = _ffn_up_prompt(xn2, p["w_gate"], p["w_up"], p["conv_w"], p["conv_b"], n_seq, tm=tm)
        conv_new = tail[:, SUBLANES - (CONV_W - 1):, :]
    else:
        h, t0, t1 = _ffn_up_sample(xn2, p["w_gate"], p["w_up"], p["conv_w"], p["conv_b"], conv_buf, n_seq)
        conv_new = jnp.stack([t0, t1], axis=1)
    down = _ffn_down(h, p["w_down"], tm)
    y = _rmsnorm(x1, p["final_norm_g"], F32, add=down)
    return y, ret_new, hg_new, conv_new


def kernel(x_prompt, x_sample, state_ret, state_hgrn, state_ffn_conv, norm_mix_g, w_in, ret_norm_g,
           hg_norm_g, hg_lb_logits, w_br_ret, w_br_hg, w_out, norm_ffn_g, w_gate, conv_w, conv_b,
           w_up, w_down, final_norm_g):
    assert norm_mix_g.shape[0] == 1, "single-layer trunk"
    bp, tp, d = x_prompt.shape
    bs, ts, _ = x_sample.shape
    p = dict(norm_mix_g=norm_mix_g[0], w_in=w_in[0], ret_norm_g=ret_norm_g[0], hg_norm_g=hg_norm_g[0],
             hg_lb_logits=hg_lb_logits, w_br_ret=w_br_ret[0], w_br_hg=w_br_hg[0], w_out=w_out[0],
             norm_ffn_g=norm_ffn_g[0], w_gate=w_gate[0], conv_w=conv_w[0], conv_b=conv_b[0],
             w_up=w_up[0], w_down=w_down[0], final_norm_g=final_norm_g)
    pos_p = jnp.arange(tp, dtype=F32)
    pos_s = PAST_LEN + jnp.arange(ts, dtype=F32)

    yp, rp, hp, cp = _trunk(x_prompt.reshape(bp * tp, d), bp, pos_p, None, None, None, p, 1, 1)
    ys, rs, hs, cs = _trunk(x_sample.reshape(bs * ts, d), bs, pos_s, state_ret[0], state_hgrn[0],
                            state_ffn_conv[0], p, 8, 8)
    return (yp.reshape(bp, tp, d), ys.reshape(bs, ts, d), rp[None], hp[None], cp[None],
            rs[None], hs[None], cs[None])
```

```python
import functools
import math

import jax
import jax.numpy as jnp
from jax import lax
from jax.experimental import pallas as pl
from jax.experimental.pallas import tpu as pltpu

F32 = jnp.float32
BF16 = jnp.bfloat16

D_MODEL = 4096
PAST_LEN = 16384
RET_HEADS = 8
RET_DK = 256
RET_DV = 256
RET_CHUNK = 128
ROPE_BASE = 10000.0
HG_HEADS = 16
HG_DK = 128
HG_DV = 128
HG_CHUNK = 16
D_FF = 11008
CONV_W = 3
EPS = 1e-6

RET_QK = RET_HEADS * RET_DK
RET_V = RET_HEADS * RET_DV
HG_K = HG_HEADS * HG_DK
HG_V = HG_HEADS * HG_DV
COL_QR = 0
COL_KR = COL_QR + RET_QK
COL_VR = COL_KR + RET_QK
COL_GR = COL_VR + RET_V
COL_FH = COL_GR + RET_V
COL_QH = COL_FH + HG_K
COL_IH = COL_QH + HG_K
COL_GH = COL_IH + HG_V
COL_GATE_RET = COL_GH + HG_V
COL_GATE_HG = COL_GATE_RET + D_MODEL
IN_COLS = COL_GATE_HG + D_MODEL

LANES = 128
SUBLANES = 8
V7X_VMEM_BYTES = 64 * 1024 * 1024
MIB = 1024 * 1024

_NT = (((1,), (1,)), ((), ()))
_TN = (((0,), (0,)), ((), ()))


def _params(semantics, vmem_bytes):
    return pltpu.CompilerParams(dimension_semantics=semantics,
                                vmem_limit_bytes=min(int(vmem_bytes), V7X_VMEM_BYTES - 4 * MIB))


def _silu(x):
    return x * jax.nn.sigmoid(x)


def _rms_kernel(x_ref, g_ref, o_ref):
    x = x_ref[...]
    y = x * lax.rsqrt(jnp.mean(x * x, axis=-1, keepdims=True) + EPS)
    o_ref[...] = (y * g_ref[...]).astype(o_ref.dtype)


def _rms_add_kernel(x_ref, d_ref, g_ref, o_ref):
    x = x_ref[...] + d_ref[...]
    y = x * lax.rsqrt(jnp.mean(x * x, axis=-1, keepdims=True) + EPS)
    o_ref[...] = (y * g_ref[...]).astype(o_ref.dtype)


def _rmsnorm(x, g, out_dtype, add=None, tr=256):
    m, d = x.shape
    row = pl.BlockSpec((tr, d), lambda i: (i, 0))
    vec = pl.BlockSpec((1, d), lambda i: (0, 0))
    ins = [x] if add is None else [x, add]
    return pl.pallas_call(
        _rms_kernel if add is None else _rms_add_kernel,
        out_shape=jax.ShapeDtypeStruct((m, d), out_dtype),
        grid=(m // tr,),
        in_specs=[row] * len(ins) + [vec],
        out_specs=row,
        compiler_params=_params(("parallel",), 2 * (len(ins) + 2) * tr * d * 4),
        name="rmsnorm",
    )(*ins, g.reshape(1, d))


def _mm_kernel(x_ref, w_ref, o_ref):
    o_ref[...] = jnp.dot(x_ref[...], w_ref[...].astype(BF16), preferred_element_type=F32)


def _in_proj(xn, w, tm, tn=512):
    m, k = xn.shape
    n = w.shape[1]
    vmem = 2 * (tm * k * 2 + k * tn * 4 + tm * tn * 4) + k * tn * 2 + tm * tn * 4
    return pl.pallas_call(
        _mm_kernel,
        out_shape=jax.ShapeDtypeStruct((m, n), F32),
        grid=(m // tm, n // tn),
        in_specs=[pl.BlockSpec((tm, k), lambda i, j: (i, 0)),
                  pl.BlockSpec((k, tn), lambda i, j: (0, j))],
        out_specs=pl.BlockSpec((tm, tn), lambda i, j: (i, j)),
        compiler_params=_params(("parallel", "arbitrary"), vmem + 4 * MIB),
        name="in_proj",
    )(xn, w)


def _merge_kernel(or_ref, oh_ref, wr_ref, wh_ref, gr_ref, gh_ref, o_ref):
    pr = jnp.dot(or_ref[...], wr_ref[...].astype(BF16), preferred_element_type=F32)
    ph = jnp.dot(oh_ref[...], wh_ref[...].astype(BF16), preferred_element_type=F32)
    o_ref[...] = (jax.nn.sigmoid(gr_ref[...]) * pr + jax.nn.sigmoid(gh_ref[...]) * ph).astype(o_ref.dtype)


def _merge(o_r, o_h, w_br_ret, w_br_hg, proj, tm, tn=512):
    m = o_r.shape[0]
    kr, kh = o_r.shape[1], o_h.shape[1]
    n = D_MODEL
    vmem = 2 * (tm * (kr + kh) * 2 + (kr + kh) * tn * 4 + 2 * tm * tn * 4 + tm * tn * 2)
    vmem += (kr + kh) * tn * 2 + 3 * tm * tn * 4
    return pl.pallas_call(
        _merge_kernel,
        out_shape=jax.ShapeDtypeStruct((m, n), BF16),
        grid=(m // tm, n // tn),
        in_specs=[pl.BlockSpec((tm, kr), lambda i, j: (i, 0)),
                  pl.BlockSpec((tm, kh), lambda i, j: (i, 0)),
                  pl.BlockSpec((kr, tn), lambda i, j: (0, j)),
                  pl.BlockSpec((kh, tn), lambda i, j: (0, j)),
                  pl.BlockSpec((tm, tn), lambda i, j: (i, COL_GATE_RET // tn + j)),
                  pl.BlockSpec((tm, tn), lambda i, j: (i, COL_GATE_HG // tn + j))],
        out_specs=pl.BlockSpec((tm, tn), lambda i, j: (i, j)),
        compiler_params=_params(("parallel", "arbitrary"), vmem + 4 * MIB),
        name="merge",
    )(o_r, o_h, w_br_ret, w_br_hg, proj, proj)


def _out_proj_kernel(m_ref, w_ref, x_ref, o_ref):
    o_ref[...] = x_ref[...] + jnp.dot(m_ref[...], w_ref[...].astype(BF16), preferred_element_type=F32)


def _out_proj(merged, w, x, tm, tn=512):
    m, k = merged.shape
    n = w.shape[1]
    vmem = 2 * (tm * k * 2 + k * tn * 4 + 2 * tm * tn * 4) + k * tn * 2 + tm * tn * 4
    return pl.pallas_call(
        _out_proj_kernel,
        out_shape=jax.ShapeDtypeStruct((m, n), F32),
        grid=(m // tm, n // tn),
        in_specs=[pl.BlockSpec((tm, k), lambda i, j: (i, 0)),
                  pl.BlockSpec((k, tn), lambda i, j: (0, j)),
                  pl.BlockSpec((tm, tn), lambda i, j: (i, j))],
        out_specs=pl.BlockSpec((tm, tn), lambda i, j: (i, j)),
        compiler_params=_params(("parallel", "arbitrary"), vmem + 4 * MIB),
        name="out_proj",
    )(merged, w, x)


def _ffn_down_kernel(h_ref, w_ref, o_ref, *, k_last):
    k = pl.program_id(1)
    n_k = pl.num_programs(1)

    @pl.when(k == 0)
    def _():
        o_ref[...] = jnp.dot(h_ref[...], w_ref[...].astype(BF16), preferred_element_type=F32)

    @pl.when(jnp.logical_and(k > 0, k < n_k - 1))
    def _():
        o_ref[...] += jnp.dot(h_ref[...], w_ref[...].astype(BF16), preferred_element_type=F32)

    @pl.when(k == n_k - 1)
    def _():
        o_ref[...] += jnp.dot(h_ref[:, :k_last], w_ref[:k_last, :].astype(BF16), preferred_element_type=F32)


def _ffn_down(h, w, tm, tk=512):
    m, f = h.shape
    n = w.shape[1]
    n_k = pl.cdiv(f, tk)
    assert n_k >= 2
    vmem = 2 * (tm * tk * 2 + tk * n * 4 + tm * n * 4) + tk * n * 2 + tm * n * 4
    return pl.pallas_call(
        functools.partial(_ffn_down_kernel, k_last=f - (n_k - 1) * tk),
        out_shape=jax.ShapeDtypeStruct((m, n), F32),
        grid=(m // tm, n_k),
        in_specs=[pl.BlockSpec((tm, tk), lambda i, k: (i, k)),
                  pl.BlockSpec((tk, n), lambda i, k: (k, 0))],
        out_specs=pl.BlockSpec((tm, n), lambda i, k: (i, 0)),
        compiler_params=_params(("parallel", "arbitrary"), vmem + 4 * MIB),
        name="ffn_down",
    )(h, w)


def _conv_gate(full0, full1, full2, cw_ref, cb_ref, up):
    c = cb_ref[...] + ((full0 * cw_ref[0:1, :] + full1 * cw_ref[1:2, :]) + full2 * cw_ref[2:3, :])
    return _silu(c) * up


def _ffn_up_prompt_kernel(xn_ref, wg_ref, wu_ref, cw_ref, cb_ref, h_ref, tail_ref,
                          wg_s, wu_s, u_s, up_s, *, tm, n_blocks, blocks_per_seq):
    i = pl.program_id(1)

    def matmuls():
        xn = xn_ref[...]
        u_s[SUBLANES:SUBLANES + tm, :] = jnp.dot(xn, wg_s[...], preferred_element_type=F32)
        up_s[...] = jnp.dot(xn, wu_s[...], preferred_element_type=F32)

    def epilogue():
        h = _conv_gate(u_s[SUBLANES - 2:SUBLANES - 2 + tm, :], u_s[SUBLANES - 1:SUBLANES - 1 + tm, :],
                       u_s[SUBLANES:SUBLANES + tm, :], cw_ref, cb_ref, up_s[...])
        h_ref[...] = h.astype(h_ref.dtype)
        last = u_s[tm:tm + SUBLANES, :]
        tail_ref[0] = last
        return last

    @pl.when(i == 0)
    def _():
        wg_s[...] = wg_ref[...].astype(BF16)
        wu_s[...] = wu_ref[...].astype(BF16)
        u_s[0:SUBLANES, :] = jnp.zeros((SUBLANES, u_s.shape[1]), F32)
        matmuls()

    @pl.when(jnp.logical_and(i > 0, i < n_blocks))
    def _():
        last = epilogue()
        u_s[0:SUBLANES, :] = jnp.where(lax.rem(i, blocks_per_seq) == 0, 0.0, last)
        matmuls()

    @pl.when(i == n_blocks)
    def _():
        epilogue()


def _ffn_up_prompt(xn, w_gate, w_up, conv_w, conv_b, n_seq, tm=1024, tf=256):
    m, d = xn.shape
    f = w_gate.shape[1]
    n_blocks = m // tm
    blocks_per_seq = n_blocks // n_seq
    vmem = 2 * (tm * d * 2 + 2 * d * tf * 4 + tm * tf * 2) + 2 * d * tf * 2 + (2 * tm + SUBLANES) * tf * 4
    vmem += 4 * tm * tf * 4
    prev = lambda i: jnp.maximum(i - 1, 0)
    return pl.pallas_call(
        functools.partial(_ffn_up_prompt_kernel, tm=tm, n_blocks=n_blocks, blocks_per_seq=blocks_per_seq),
        out_shape=(jax.ShapeDtypeStruct((m, f), BF16),
                   jax.ShapeDtypeStruct((n_seq, SUBLANES, f), F32)),
        grid=(f // tf, n_blocks + 1),
        in_specs=[pl.BlockSpec((tm, d), lambda j, i: (jnp.minimum(i, n_blocks - 1), 0)),
                  pl.BlockSpec((d, tf), lambda j, i: (0, j)),
                  pl.BlockSpec((d, tf), lambda j, i: (0, j)),
                  pl.BlockSpec((CONV_W, tf), lambda j, i: (0, j)),
                  pl.BlockSpec((1, tf), lambda j, i: (0, j))],
        out_specs=(pl.BlockSpec((tm, tf), lambda j, i: (prev(i), j)),
                   pl.BlockSpec((1, SUBLANES, tf), lambda j, i: (prev(i) // blocks_per_seq, 0, j))),
        scratch_shapes=[pltpu.VMEM((d, tf), BF16), pltpu.VMEM((d, tf), BF16),
                        pltpu.VMEM((tm + SUBLANES, tf), F32), pltpu.VMEM((tm, tf), F32)],
        compiler_params=_params(("arbitrary", "arbitrary"), vmem + 4 * MIB),
        name="ffn_up_prompt",
    )(xn, w_gate, w_up, conv_w, conv_b.reshape(1, f))


def _ffn_up_sample_kernel(xn_ref, wg_ref, wu_ref, cw_ref, cb_ref, b0_ref, b1_ref,
                          h_ref, t0_ref, t1_ref, u_s, up_s, h_s, *, n_seq, t_len):
    xn = xn_ref[...]
    u = jnp.dot(xn, wg_ref[...].astype(BF16), preferred_element_type=F32)
    up = jnp.dot(xn, wu_ref[...].astype(BF16), preferred_element_type=F32)
    for l in range(u_s.shape[0]):
        ls = slice(l * LANES, (l + 1) * LANES)
        u_s[l] = u[:, ls]
        up_s[l] = up[:, ls]
        full = [b0_ref[:, ls], b1_ref[:, ls]]
        full += [u_s[l, pl.ds(t, n_seq, stride=t_len), :] for t in range(t_len)]
        for t in range(t_len):
            c = cb_ref[:, ls] + ((full[t] * cw_ref[0:1, ls] + full[t + 1] * cw_ref[1:2, ls])
                                 + full[t + 2] * cw_ref[2:3, ls])
            h_s[l, pl.ds(t, n_seq, stride=t_len), :] = _silu(c) * up_s[l, pl.ds(t, n_seq, stride=t_len), :]
        h_ref[:, ls] = h_s[l].astype(h_ref.dtype)
        t0_ref[:, ls] = full[t_len]
        t1_ref[:, ls] = full[t_len + 1]


def _ffn_up_sample(xn, w_gate, w_up, conv_w, conv_b, conv_buf, n_seq, tf=256):
    m, d = xn.shape
    f = w_gate.shape[1]
    t_len = m // n_seq
    buf2d = conv_buf.reshape(n_seq, (CONV_W - 1) * f)
    vmem = 2 * (m * d * 2 + 2 * d * tf * 4 + m * tf * 2 + 4 * n_seq * tf * 4) + 2 * d * tf * 2 + 5 * m * tf * 4
    return pl.pallas_call(
        functools.partial(_ffn_up_sample_kernel, n_seq=n_seq, t_len=t_len),
        out_shape=(jax.ShapeDtypeStruct((m, f), BF16),
                   jax.ShapeDtypeStruct((n_seq, f), F32),
                   jax.ShapeDtypeStruct((n_seq, f), F32)),
        grid=(f // tf,),
        in_specs=[pl.BlockSpec((m, d), lambda j: (0, 0)),
                  pl.BlockSpec((d, tf), lambda j: (0, j)),
                  pl.BlockSpec((d, tf), lambda j: (0, j)),
                  pl.BlockSpec((CONV_W, tf), lambda j: (0, j)),
                  pl.BlockSpec((1, tf), lambda j: (0, j)),
                  pl.BlockSpec((n_seq, tf), lambda j: (0, j)),
                  pl.BlockSpec((n_seq, tf), lambda j: (0, f // tf + j))],
        out_specs=(pl.BlockSpec((m, tf), lambda j: (0, j)),
                   pl.BlockSpec((n_seq, tf), lambda j: (0, j)),
                   pl.BlockSpec((n_seq, tf), lambda j: (0, j))),
        scratch_shapes=[pltpu.VMEM((tf // LANES, m, LANES), F32)] * 3,
        compiler_params=_params(("parallel",), vmem + 4 * MIB),
        name="ffn_up_sample",
    )(xn, w_gate, w_up, conv_w, conv_b.reshape(1, f), buf2d, buf2d)


def _retention_kernel(lg_ref, q_ref, k_ref, v_ref, g_ref, cos_ref, sin_ref, gn_ref, *rest,
                      chunk, n_seg, heads, carry):
    if carry:
        o_ref, sout_ref = rest
    else:
        s0_ref, o_ref, sout_ref = rest
    rows = chunk * n_seg
    shift = int(math.log2(chunk))
    half = RET_DK // 2
    cos = cos_ref[...]
    sin = sin_ref[...]

    def rot(x):
        x1 = x[:, :half]
        x2 = x[:, half:]
        return jnp.concatenate([x1 * cos - x2 * sin, x1 * sin + x2 * cos], axis=-1)

    ri = lax.broadcasted_iota(jnp.int32, (rows, rows), 0)
    ci = lax.broadcasted_iota(jnp.int32, (rows, rows), 1)
    ok = ri >= ci
    if n_seg > 1:
        ok = jnp.logical_and(ok, lax.shift_right_logical(ri, shift) == lax.shift_right_logical(ci, shift))
    diff = jnp.where(ok, (ri - ci).astype(F32), 0.0)
    row_id = lax.broadcasted_iota(jnp.int32, (rows, RET_DK), 0)
    pos = (row_id & (chunk - 1)).astype(F32)
    seg = lax.shift_right_logical(row_id, shift)

    if carry:
        @pl.when(pl.program_id(2) == 0)
        def _():
            sout_ref[...] = jnp.zeros_like(sout_ref)

    for h in range(heads):
        cs = slice(h * RET_DK, (h + 1) * RET_DK)
        log_g = lg_ref[pl.program_id(1) * heads + h]
        dmat = jnp.where(ok, jnp.exp(diff * log_g), 0.0)
        q_dec = jnp.exp((pos + 1.0) * log_g)
        k_dec = jnp.exp((chunk - 1.0 - pos) * log_g)
        s_dec = jnp.exp(jnp.full((1, RET_DV), float(chunk), F32) * log_g)

        qb = rot(q_ref[:, cs]).astype(BF16)
        kr = rot(k_ref[:, cs]) * (RET_DK ** -0.5)
        vb = v_ref[:, cs].astype(BF16)
        scores = lax.dot_general(qb, kr.astype(BF16), _NT, preferred_element_type=F32) * dmat
        o = jnp.dot(scores.astype(BF16), vb, preferred_element_type=F32)
        kd = kr * k_dec
        if carry:
            s = sout_ref[0, h]
            o = o + jnp.dot(qb, s.astype(BF16), preferred_element_type=F32) * q_dec
            sout_ref[0, h] = s * s_dec + lax.dot_general(kd.astype(BF16), vb, _TN, preferred_element_type=F32)
        else:
            for b in range(n_seg):
                mine = seg == b
                s = s0_ref[b, h]
                o_b = jnp.dot(qb, s.astype(BF16), preferred_element_type=F32) * q_dec
                o = o + jnp.where(mine, o_b, 0.0)
                kd_b = jnp.where(mine, kd, 0.0).astype(BF16)
                sout_ref[b, h] = s * s_dec + lax.dot_general(kd_b, vb, _TN, preferred_element_type=F32)

        y = o * lax.rsqrt(jnp.mean(o * o, axis=-1, keepdims=True) + EPS) * gn_ref[:, cs]
        o_ref[:, cs] = (y * _silu(g_ref[:, cs])).astype(o_ref.dtype)


def _retention(proj, log_g, cos, sin, ret_norm_g, state, n_seq, chunk, n_seg, heads):
    m = proj.shape[0]
    rows = chunk * n_seg
    carry = state is None
    width = heads * RET_DK
    cq, ck, cv, cg = (c // width for c in (COL_QR, COL_KR, COL_VR, COL_GR))
    n_hg = RET_HEADS // heads
    if carry:
        n_chunks = m // n_seq // rows
        grid = (n_seq, n_hg, n_chunks)
        row_blk = lambda b, hg, c, lg: b * n_chunks + c
        tab_blk = lambda b, hg, c, lg: (c, 0)
        st_blk = lambda b, hg, c, lg: (b, hg, 0, 0)
        sem = ("parallel", "parallel", "arbitrary")
    else:
        grid = (m // rows, n_hg)
        row_blk = lambda i, hg, lg: i
        tab_blk = lambda i, hg, lg: (0, 0)
        st_blk = lambda i, hg, lg: (i, hg, 0, 0)
        sem = ("parallel", "parallel")
    st_shape = (n_seg, heads, RET_DK, RET_DV)

    def col(c0):
        return pl.BlockSpec((rows, width), lambda *a: (row_blk(*a), c0 + a[1]))

    in_specs = [col(cq), col(ck), col(cv), col(cg),
                pl.BlockSpec((rows, RET_DK // 2), tab_blk),
                pl.BlockSpec((rows, RET_DK // 2), tab_blk),
                pl.BlockSpec((1, width), lambda *a: (0, a[1]))]
    args = [proj, proj, proj, proj, cos, sin, ret_norm_g.reshape(1, RET_V)]
    if not carry:
        in_specs.append(pl.BlockSpec(st_shape, st_blk))
        args.append(state)
    st_bytes = n_seg * heads * RET_DK * RET_DV * 4
    vmem = 2 * (6 * rows * width * 4 + 2 * st_bytes) + 16 * rows * RET_DK * 4 + 4 * RET_DK * RET_DV * 4
    return pl.pallas_call(
        functools.partial(_retention_kernel, chunk=chunk, n_seg=n_seg, heads=heads, carry=carry),
        out_shape=(jax.ShapeDtypeStruct((m, RET_V), BF16),
                   jax.ShapeDtypeStruct((n_seq, RET_HEADS, RET_DK, RET_DV), F32)),
        grid_spec=pltpu.PrefetchScalarGridSpec(
            num_scalar_prefetch=1, grid=grid, in_specs=in_specs,
            out_specs=(pl.BlockSpec((rows, width), lambda *a: (row_blk(*a), a[1])),
                       pl.BlockSpec(st_shape, st_blk))),
        compiler_params=_params(sem, vmem + 8 * MIB),
        name="retention_prompt" if carry else "retention_sample",
    )(log_g, *args)


def _hg_gates(f_raw, i_raw, lb):
    f = lb + (1.0 - lb) * jax.nn.sigmoid(f_raw)
    return jnp.log(f), 1.0 - f, _silu(i_raw)


def _hg_lower_bound(lb_ref):
    logits = lb_ref[...]
    e = jnp.exp(logits - jnp.max(logits, axis=0, keepdims=True))
    return e[0:1, :] / jnp.sum(e, axis=0, keepdims=True)


def _hg_cumsum(log_f, pos, chunk):
    b = log_f
    sh = 1
    while sh < chunk:
        b = b + jnp.where(pos >= sh, pltpu.roll(b, sh, 0), 0.0)
        sh *= 2
    return b


def _hg_intra(q, k, v, b, pos, chunk):
    o = jnp.sum(q * k, axis=-1, keepdims=True) * v
    for d in range(1, chunk):
        valid = pos >= d
        decay = jnp.exp(jnp.where(valid, b - pltpu.roll(b, d, 0), -jnp.inf))
        score = jnp.sum(q * decay * pltpu.roll(k, d, 0), axis=-1, keepdims=True)
        o = o + score * pltpu.roll(v, d, 0)
    return o


def _hg_finish(o, g_raw, gn):
    y = o * lax.rsqrt(jnp.mean(o * o, axis=-1, keepdims=True) + EPS) * gn
    return y * _silu(g_raw)


def _hg_intra_scores(q, k, b2, chunk, lane, row):
    n_blk = chunk // SUBLANES
    qs = [q[i * SUBLANES:(i + 1) * SUBLANES] for i in range(n_blk)]
    bs = [b2[i * SUBLANES:(i + 1) * SUBLANES] for i in range(n_blk)]
    c2 = b2 - jnp.log2(jnp.maximum(k, 0.0))
    p = [jnp.zeros((SUBLANES, HG_DK), F32) for _ in range(n_blk)]
    for s in range(chunk):
        c_s = jnp.broadcast_to(c2[s:s + 1, :], (SUBLANES, HG_DK))
        for i in range(s // SUBLANES, n_blk):
            col = jnp.sum(qs[i] * jnp.exp2(bs[i] - c_s), axis=-1, keepdims=True)
            sel = lane == s
            if i == s // SUBLANES:
                sel = jnp.logical_and(sel, row >= (s - i * SUBLANES))
            p[i] = jnp.where(sel, col, p[i])
    return jnp.concatenate(p, axis=0)


def _hgrn_prompt_kernel(f_ref, q_ref, i_ref, g_ref, lb_ref, gn_ref, o_ref, sout_ref,
                        st_ref, p_s, qd_s, kd_s, v_s, e_s, *, heads, rows, chunk):
    tb = pl.program_id(2)

    @pl.when(tb == 0)
    def _():
        st_ref[...] = jnp.zeros_like(st_ref)

    lb = _hg_lower_bound(lb_ref)
    gn = gn_ref[...]
    pos = lax.broadcasted_iota(jnp.int32, (chunk, HG_DK), 0)
    lane = lax.broadcasted_iota(jnp.int32, (SUBLANES, HG_DK), 1)
    row = lax.broadcasted_iota(jnp.int32, (SUBLANES, HG_DK), 0)

    def front(c):
        rs = pl.ds(pl.multiple_of(c * chunk, chunk), chunk)
        for h in range(heads):
            cs = slice(h * HG_DK, (h + 1) * HG_DK)
            q = q_ref[rs, cs]
            f = lb[:, cs] + (1.0 - lb[:, cs]) * jax.nn.sigmoid(f_ref[rs, cs])
            k = 1.0 - f
            b2 = _hg_cumsum(jnp.log2(f), pos, chunk)
            p_s[h] = _hg_intra_scores(q, k, b2, chunk, lane, row).astype(BF16)
            b_last = b2[chunk - 1:chunk, :]
            qd_s[h] = (q * jnp.exp2(b2)).astype(BF16)
            kd_s[h] = (k * jnp.exp2(b_last - b2)).astype(BF16)
            v_s[h] = _silu(i_ref[rs, cs]).astype(BF16)
            e_s[h] = jnp.broadcast_to(jnp.exp2(b_last), (SUBLANES, HG_DK))

    def back(c):
        rs = pl.ds(pl.multiple_of(c * chunk, chunk), chunk)
        for h in range(heads):
            cs = slice(h * HG_DK, (h + 1) * HG_DK)
            vb = v_s[h]
            st = st_ref[h]
            o = jnp.dot(p_s[h][:, :chunk], vb, preferred_element_type=F32)
            o = o + lax.dot_general(qd_s[h], st.astype(BF16), _NT, preferred_element_type=F32)
            st_ref[h] = st * e_s[h][0:1, :] + lax.dot_general(vb, kd_s[h], _TN, preferred_element_type=F32)
            o_ref[rs, cs] = _hg_finish(o, g_ref[rs, cs], gn[:, cs]).astype(o_ref.dtype)

    def body(c, carry):
        back(c - 1)
        front(c)
        return carry

    n_chunks = rows // chunk
    front(0)
    lax.fori_loop(1, n_chunks, body, 0)
    back(n_chunks - 1)

    @pl.when(tb == pl.num_programs(2) - 1)
    def _():
        for h in range(heads):
            sout_ref[0, h] = st_ref[h].T


def _hgrn_prompt(proj, lb_logits, hg_norm_g, n_seq, chunk, heads=8, rows=1024):
    m = proj.shape[0]
    n_tb = m // n_seq // rows
    width = heads * HG_DK
    cf, cq, ci, cg = (c // width for c in (COL_FH, COL_QH, COL_IH, COL_GH))

    def col(c0):
        return pl.BlockSpec((rows, width), lambda b, hg, t: (b * n_tb + t, c0 + hg))

    vmem = 2 * (4 * rows * width * 4 + rows * width * 2 + heads * HG_DK * HG_DV * 4) + heads * HG_DK * HG_DV * 4
    return pl.pallas_call(
        functools.partial(_hgrn_prompt_kernel, heads=heads, rows=rows, chunk=chunk),
        out_shape=(jax.ShapeDtypeStruct((m, HG_V), BF16),
                   jax.ShapeDtypeStruct((n_seq, HG_HEADS, HG_DK, HG_DV), F32)),
        grid=(n_seq, HG_HEADS // heads, n_tb),
        in_specs=[col(cf), col(cq), col(ci), col(cg),
                  pl.BlockSpec((lb_logits.shape[0], width), lambda b, hg, t: (0, hg)),
                  pl.BlockSpec((1, width), lambda b, hg, t: (0, hg))],
        out_specs=(pl.BlockSpec((rows, width), lambda b, hg, t: (b * n_tb + t, hg)),
                   pl.BlockSpec((1, heads, HG_DK, HG_DV), lambda b, hg, t: (b, hg, 0, 0))),
        scratch_shapes=[pltpu.VMEM((heads, HG_DV, HG_DK), F32)]
                       + [pltpu.VMEM((heads, chunk, HG_DK), BF16)] * 4
                       + [pltpu.VMEM((heads, SUBLANES, HG_DK), F32)],
        compiler_params=_params(("parallel", "parallel", "arbitrary"), vmem + 8 * MIB),
        name="hgrn_prompt",
    )(proj, proj, proj, proj, lb_logits, hg_norm_g.reshape(1, HG_V))


def _hgrn_sample_kernel(f_ref, q_ref, i_ref, g_ref, lb_ref, gn_ref, s0_ref, o_ref, sout_ref,
                        *, heads, n_seg, chunk):
    rows = n_seg * chunk
    shift = int(math.log2(chunk))
    lb = _hg_lower_bound(lb_ref)
    gn = gn_ref[...]
    row_id = lax.broadcasted_iota(jnp.int32, (rows, HG_DK), 0)
    pos = row_id & (chunk - 1)
    seg = lax.shift_right_logical(row_id, shift)
    eye = (lax.broadcasted_iota(jnp.int32, (HG_DK, HG_DK), 0)
           == lax.broadcasted_iota(jnp.int32, (HG_DK, HG_DK), 1))
    for h in range(heads):
        cs = slice(h * HG_DK, (h + 1) * HG_DK)
        q = q_ref[:, cs]
        log_f, k, v = _hg_gates(f_ref[:, cs], i_ref[:, cs], lb[:, cs])
        b = _hg_cumsum(log_f, pos, chunk)
        o = _hg_intra(q, k, v, b, pos, chunk)
        b_last = b
        for j in range(1, chunk):
            b_last = jnp.where(pos == chunk - 1 - j, pltpu.roll(b, rows - j, 0), b_last)
        q_dec = (q * jnp.exp(b)).astype(BF16)
        k_dec = k * jnp.exp(b_last - b)
        vb = v.astype(BF16)
        for s in range(n_seg):
            mine = seg == s
            st = s0_ref[s, h]
            o = o + jnp.where(mine, jnp.dot(q_dec, st.astype(BF16), preferred_element_type=F32), 0.0)
            e_row = jnp.exp(b[(s + 1) * chunk - 1:(s + 1) * chunk, :])
            e_col = jnp.sum(jnp.where(eye, e_row, 0.0), axis=-1, keepdims=True)
            k_s = jnp.where(mine, k_dec, 0.0).astype(BF16)
            sout_ref[s, h] = e_col * st + lax.dot_general(k_s, vb, _TN, preferred_element_type=F32)
        o_ref[:, cs] = _hg_finish(o, g_ref[:, cs], gn[:, cs]).astype(o_ref.dtype)


def _hgrn_sample(proj, lb_logits, hg_norm_g, state, chunk, heads=8, n_seg=8):
    m = proj.shape[0]
    n_seq = m // chunk
    rows = n_seg * chunk
    width = heads * HG_DK
    cf, cq, ci, cg = (c // width for c in (COL_FH, COL_QH, COL_IH, COL_GH))

    def col(c0):
        return pl.BlockSpec((rows, width), lambda i, hg: (i, c0 + hg))

    st_spec = pl.BlockSpec((n_seg, heads, HG_DK, HG_DV), lambda i, hg: (i, hg, 0, 0))
    st_bytes = n_seg * heads * HG_DK * HG_DV * 4
    vmem = 2 * (5 * rows * width * 4 + 2 * st_bytes)
    return pl.pallas_call(
        functools.partial(_hgrn_sample_kernel, heads=heads, n_seg=n_seg, chunk=chunk),
        out_shape=(jax.ShapeDtypeStruct((m, HG_V), BF16),
                   jax.ShapeDtypeStruct((n_seq, HG_HEADS, HG_DK, HG_DV), F32)),
        grid=(m // rows, HG_HEADS // heads),
        in_specs=[col(cf), col(cq), col(ci), col(cg),
                  pl.BlockSpec((lb_logits.shape[0], width), lambda i, hg: (0, hg)),
                  pl.BlockSpec((1, width), lambda i, hg: (0, hg)),
                  st_spec],
        out_specs=(pl.BlockSpec((rows, width), lambda i, hg: (i, hg)), st_spec),
        compiler_params=_params(("parallel", "parallel"), vmem + 8 * MIB),
        name="hgrn_sample",
    )(proj, proj, proj, proj, lb_logits, hg_norm_g.reshape(1, HG_V), state)


def _rope_tables(pos):
    half = RET_DK // 2
    inv = ROPE_BASE ** (-jnp.arange(half, dtype=F32) / half)
    ang = pos[:, None] * inv[None, :]
    return jnp.cos(ang), jnp.sin(ang)


def _trunk(x, n_seq, pos, ret_state, hg_state, conv_buf, p, ret_seg, hg_seg):
    m = x.shape[0]
    t_len = m // n_seq
    prompt = ret_state is None
    tm = min(m, 1024)
    ret_chunk = math.gcd(t_len, RET_CHUNK)
    hg_chunk = math.gcd(t_len, HG_CHUNK)
    log_g = jnp.log1p(-jnp.exp2(-5.0 - jnp.arange(RET_HEADS, dtype=F32)))
    cos, sin = _rope_tables(pos)
    if not prompt:
        cos = jnp.tile(cos, (ret_seg, 1))
        sin = jnp.tile(sin, (ret_seg, 1))

    xn = _rmsnorm(x, p["norm_mix_g"], BF16)
    proj = _in_proj(xn, p["w_in"], tm)
    o_r, ret_new = _retention(proj, log_g, cos, sin, p["ret_norm_g"], ret_state, n_seq, ret_chunk,
                              1 if prompt else ret_seg, RET_HEADS if prompt else 1)
    if prompt:
        o_h, hg_new = _hgrn_prompt(proj, p["hg_lb_logits"], p["hg_norm_g"], n_seq, hg_chunk)
    else:
        o_h, hg_new = _hgrn_sample(proj, p["hg_lb_logits"], p["hg_norm_g"], hg_state, hg_chunk, n_seg=hg_seg)
    merged = _merge(o_r, o_h, p["w_br_ret"], p["w_br_hg"], proj, tm)
    x1 = _out_proj(merged, p["w_out"], x, tm)

    xn2 = _rmsnorm(x1, p["norm_ffn_g"], BF16)
    if prompt:
        h, tail = _ffn_up_prompt(xn2, p["w_gate"], p["w_up"], p["conv_w"], p["conv_b"], n_seq, tm=tm)
        conv_new = tail[:, SUBLANES - (CONV_W - 1):, :]
    else:
        h, t0, t1 = _ffn_up_sample(xn2, p["w_gate"], p["w_up"], p["conv_w"], p["conv_b"], conv_buf, n_seq)
        conv_new = jnp.stack([t0, t1], axis=1)
    down = _ffn_down(h, p["w_down"], tm)
    y = _rmsnorm(x1, p["final_norm_g"], F32, add=down)
    return y, ret_new, hg_new, conv_new


def kernel(x_prompt, x_sample, state_ret, state_hgrn, state_ffn_conv, norm_mix_g, w_in, ret_norm_g,
           hg_norm_g, hg_lb_logits, w_br_ret, w_br_hg, w_out, norm_ffn_g, w_gate, conv_w, conv_b,
           w_up, w_down, final_norm_g):
    assert norm_mix_g.shape[0] == 1, "single-layer trunk"
    bp, tp, d = x_prompt.shape
    bs, ts, _ = x_sample.shape
    p = dict(norm_mix_g=norm_mix_g[0], w_in=w_in[0], ret_norm_g=ret_norm_g[0], hg_norm_g=hg_norm_g[0],
             hg_lb_logits=hg_lb_logits, w_br_ret=w_br_ret[0], w_br_hg=w_br_hg[0], w_out=w_out[0],
             norm_ffn_g=norm_ffn_g[0], w_gate=w_gate[0], conv_w=conv_w[0], conv_b=conv_b[0],
             w_up=w_up[0], w_down=w_down[0], final_norm_g=final_norm_g)
    pos_p = jnp.arange(tp, dtype=F32)
    pos_s = PAST_LEN + jnp.arange(ts, dtype=F32)

    yp, rp, hp, cp = _trunk(x_prompt.reshape(bp * tp, d), bp, pos_p, None, None, None, p, 1, 1)
    ys, rs, hs, cs = _trunk(x_sample.reshape(bs * ts, d), bs, pos_s, state_ret[0], state_hgrn[0],
                            state_ffn_conv[0], p, 8, 8)
    return (yp.reshape(bp, tp, d), ys.reshape(bs, ts, d), rp[None], hp[None], cp[None],
            rs[None], hs[None], cs[None])
```

```python
import functools
import math

import jax
import jax.numpy as jnp
from jax import lax
from jax.experimental import pallas as pl
from jax.experimental.pallas import tpu as pltpu

F32 = jnp.float32
BF16 = jnp.bfloat16

D_MODEL = 4096
PAST_LEN = 16384
RET_HEADS = 8
RET_DK = 256
RET_DV = 256
RET_CHUNK = 128
ROPE_BASE = 10000.0
HG_HEADS = 16
HG_DK = 128
HG_DV = 128
HG_CHUNK = 16
D_FF = 11008
CONV_W = 3
EPS = 1e-6

RET_QK = RET_HEADS * RET_DK
RET_V = RET_HEADS * RET_DV
HG_K = HG_HEADS * HG_DK
HG_V = HG_HEADS * HG_DV
COL_QR = 0
COL_KR = COL_QR + RET_QK
COL_VR = COL_KR + RET_QK
COL_GR = COL_VR + RET_V
COL_FH = COL_GR + RET_V
COL_QH = COL_FH + HG_K
COL_IH = COL_QH + HG_K
COL_GH = COL_IH + HG_V
COL_GATE_RET = COL_GH + HG_V
COL_GATE_HG = COL_GATE_RET + D_MODEL
IN_COLS = COL_GATE_HG + D_MODEL

LANES = 128
SUBLANES = 8
V7X_MXU_DIM = 256
V7X_VMEM_BYTES = 64 * 1024 * 1024
MIB = 1024 * 1024

_NT = (((1,), (1,)), ((), ()))
_TN = (((0,), (0,)), ((), ()))


def _params(semantics, vmem_bytes):
    return pltpu.CompilerParams(dimension_semantics=semantics,
                                vmem_limit_bytes=min(int(vmem_bytes), V7X_VMEM_BYTES - 4 * MIB))


def _silu(x):
    return x * jax.nn.sigmoid(x)


def _rms_kernel(x_ref, g_ref, o_ref):
    x = x_ref[...]
    y = x * lax.rsqrt(jnp.mean(x * x, axis=-1, keepdims=True) + EPS)
    o_ref[...] = (y * g_ref[...]).astype(o_ref.dtype)


def _rms_add_kernel(x_ref, d_ref, g_ref, o_ref):
    x = x_ref[...] + d_ref[...]
    y = x * lax.rsqrt(jnp.mean(x * x, axis=-1, keepdims=True) + EPS)
    o_ref[...] = (y * g_ref[...]).astype(o_ref.dtype)


def _rmsnorm(x, g, out_dtype, add=None, tr=256):
    m, d = x.shape
    row = pl.BlockSpec((tr, d), lambda i: (i, 0))
    vec = pl.BlockSpec((1, d), lambda i: (0, 0))
    ins = [x] if add is None else [x, add]
    return pl.pallas_call(
        _rms_kernel if add is None else _rms_add_kernel,
        out_shape=jax.ShapeDtypeStruct((m, d), out_dtype),
        grid=(m // tr,),
        in_specs=[row] * len(ins) + [vec],
        out_specs=row,
        compiler_params=_params(("parallel",), 2 * (len(ins) + 2) * tr * d * 4),
        name="rmsnorm",
    )(*ins, g.reshape(1, d))


def _mm_kernel(x_ref, w_ref, o_ref):
    o_ref[...] = jnp.dot(x_ref[...], w_ref[...].astype(BF16), preferred_element_type=F32)


def _in_proj(xn, w, tm, tn=512):
    m, k = xn.shape
    n = w.shape[1]
    vmem = 2 * (tm * k * 2 + k * tn * 4 + tm * tn * 4) + k * tn * 2 + tm * tn * 4
    return pl.pallas_call(
        _mm_kernel,
        out_shape=jax.ShapeDtypeStruct((m, n), F32),
        grid=(m // tm, n // tn),
        in_specs=[pl.BlockSpec((tm, k), lambda i, j: (i, 0)),
                  pl.BlockSpec((k, tn), lambda i, j: (0, j))],
        out_specs=pl.BlockSpec((tm, tn), lambda i, j: (i, j)),
        compiler_params=_params(("parallel", "arbitrary"), vmem + 4 * MIB),
        name="in_proj",
    )(xn, w)


def _merge_kernel(or_ref, oh_ref, wr_ref, wh_ref, gr_ref, gh_ref, o_ref):
    pr = jnp.dot(or_ref[...], wr_ref[...].astype(BF16), preferred_element_type=F32)
    ph = jnp.dot(oh_ref[...], wh_ref[...].astype(BF16), preferred_element_type=F32)
    o_ref[...] = (jax.nn.sigmoid(gr_ref[...]) * pr + jax.nn.sigmoid(gh_ref[...]) * ph).astype(o_ref.dtype)


def _merge(o_r, o_h, w_br_ret, w_br_hg, proj, tm, tn=512):
    m = o_r.shape[0]
    kr, kh = o_r.shape[1], o_h.shape[1]
    n = D_MODEL
    vmem = 2 * (tm * (kr + kh) * 2 + (kr + kh) * tn * 4 + 2 * tm * tn * 4 + tm * tn * 2)
    vmem += (kr + kh) * tn * 2 + 3 * tm * tn * 4
    return pl.pallas_call(
        _merge_kernel,
        out_shape=jax.ShapeDtypeStruct((m, n), BF16),
        grid=(m // tm, n // tn),
        in_specs=[pl.BlockSpec((tm, kr), lambda i, j: (i, 0)),
                  pl.BlockSpec((tm, kh), lambda i, j: (i, 0)),
                  pl.BlockSpec((kr, tn), lambda i, j: (0, j)),
                  pl.BlockSpec((kh, tn), lambda i, j: (0, j)),
                  pl.BlockSpec((tm, tn), lambda i, j: (i, COL_GATE_RET // tn + j)),
                  pl.BlockSpec((tm, tn), lambda i, j: (i, COL_GATE_HG // tn + j))],
        out_specs=pl.BlockSpec((tm, tn), lambda i, j: (i, j)),
        compiler_params=_params(("parallel", "arbitrary"), vmem + 4 * MIB),
        name="merge",
    )(o_r, o_h, w_br_ret, w_br_hg, proj, proj)


def _out_proj_kernel(m_ref, w_ref, x_ref, o_ref):
    o_ref[...] = x_ref[...] + jnp.dot(m_ref[...], w_ref[...].astype(BF16), preferred_element_type=F32)


def _out_proj(merged, w, x, tm, tn=512):
    m, k = merged.shape
    n = w.shape[1]
    vmem = 2 * (tm * k * 2 + k * tn * 4 + 2 * tm * tn * 4) + k * tn * 2 + tm * tn * 4
    return pl.pallas_call(
        _out_proj_kernel,
        out_shape=jax.ShapeDtypeStruct((m, n), F32),
        grid=(m // tm, n // tn),
        in_specs=[pl.BlockSpec((tm, k), lambda i, j: (i, 0)),
                  pl.BlockSpec((k, tn), lambda i, j: (0, j)),
                  pl.BlockSpec((tm, tn), lambda i, j: (i, j))],
        out_specs=pl.BlockSpec((tm, tn), lambda i, j: (i, j)),
        compiler_params=_params(("parallel", "arbitrary"), vmem + 4 * MIB),
        name="out_proj",
    )(merged, w, x)


def _ffn_down_kernel(h_ref, w_ref, o_ref, *, k_last):
    k = pl.program_id(1)
    n_k = pl.num_programs(1)

    @pl.when(k == 0)
    def _():
        o_ref[...] = jnp.dot(h_ref[...], w_ref[...].astype(BF16), preferred_element_type=F32)

    @pl.when(jnp.logical_and(k > 0, k < n_k - 1))
    def _():
        o_ref[...] += jnp.dot(h_ref[...], w_ref[...].astype(BF16), preferred_element_type=F32)

    @pl.when(k == n_k - 1)
    def _():
        o_ref[...] += jnp.dot(h_ref[:, :k_last], w_ref[:k_last, :].astype(BF16), preferred_element_type=F32)


def _ffn_down(h, w, tm, tk=512):
    m, f = h.shape
    n = w.shape[1]
    n_k = pl.cdiv(f, tk)
    assert n_k >= 2
    vmem = 2 * (tm * tk * 2 + tk * n * 4 + tm * n * 4) + tk * n * 2 + tm * n * 4
    return pl.pallas_call(
        functools.partial(_ffn_down_kernel, k_last=f - (n_k - 1) * tk),
        out_shape=jax.ShapeDtypeStruct((m, n), F32),
        grid=(m // tm, n_k),
        in_specs=[pl.BlockSpec((tm, tk), lambda i, k: (i, k)),
                  pl.BlockSpec((tk, n), lambda i, k: (k, 0))],
        out_specs=pl.BlockSpec((tm, n), lambda i, k: (i, 0)),
        compiler_params=_params(("parallel", "arbitrary"), vmem + 4 * MIB),
        name="ffn_down",
    )(h, w)


def _conv_gate(full0, full1, full2, cw_ref, cb_ref, up):
    c = cb_ref[...] + ((full0 * cw_ref[0:1, :] + full1 * cw_ref[1:2, :]) + full2 * cw_ref[2:3, :])
    return _silu(c) * up


def _zero_after(x):
    r = jnp.max(x, axis=0, keepdims=True).astype(F32)
    bits = pltpu.bitcast(r, jnp.uint32)
    bits = lax.shift_right_logical(lax.shift_right_logical(bits, jnp.uint32(16)), jnp.uint32(16))
    return pltpu.bitcast(bits, F32).astype(BF16)


def _ffn_up_prompt_kernel(xn_ref, wg_ref, wu_ref, cw_ref, cb_ref, h_ref, tail_ref,
                          w_s, u_s, up_s, *, tm, tf, n_blocks, n_steps, blocks_per_seq):
    t = pl.program_id(0)
    i = lax.rem(t, n_blocks)
    n_kt = xn_ref.shape[1] // V7X_MXU_DIM
    rc = tm // n_kt

    def store_result(res):
        u_s[SUBLANES:SUBLANES + tm, :] = res[:, :tf]
        up_s[...] = res[:, tf:]

    def epilogue_rows(r0, nrows):
        h = _conv_gate(u_s[SUBLANES - 2 + r0:SUBLANES - 2 + r0 + nrows, :],
                       u_s[SUBLANES - 1 + r0:SUBLANES - 1 + r0 + nrows, :],
                       u_s[SUBLANES + r0:SUBLANES + r0 + nrows, :], cw_ref, cb_ref,
                       up_s[r0:r0 + nrows, :]).astype(h_ref.dtype)
        h_ref[r0:r0 + nrows, :] = h
        return h

    def finish_epilogue():
        last = u_s[tm:tm + SUBLANES, :]
        tail_ref[0] = last
        return last

    def cast_weights():
        w_s[:, :tf] = wg_ref[...].astype(BF16)
        w_s[:, tf:] = wu_ref[...].astype(BF16)

    @pl.when(t == 0)
    def _():
        cast_weights()
        u_s[0:SUBLANES, :] = jnp.zeros((SUBLANES, u_s.shape[1]), F32)
        store_result(jnp.dot(xn_ref[...], w_s[...], preferred_element_type=F32))

    def steady(first_of_tile):
        if first_of_tile:
            cast_weights()
        pieces = []
        for kt in range(n_kt):
            zero = _zero_after(epilogue_rows(kt * rc, rc))
            w_kt = w_s[kt * V7X_MXU_DIM:(kt + 1) * V7X_MXU_DIM, :]
            pieces.append(w_kt + jnp.concatenate([zero, zero], axis=1))
        last = finish_epilogue()
        res = jnp.dot(xn_ref[...], jnp.concatenate(pieces, axis=0), preferred_element_type=F32)
        u_s[0:SUBLANES, :] = jnp.where(lax.rem(i, blocks_per_seq) == 0, 0.0, last)
        store_result(res)

    @pl.when(jnp.logical_and(t > 0, jnp.logical_and(t < n_steps - 1, i == 0)))
    def _():
        steady(True)

    @pl.when(jnp.logical_and(t < n_steps - 1, i > 0))
    def _():
        steady(False)

    @pl.when(t == n_steps - 1)
    def _():
        epilogue_rows(0, tm)
        finish_epilogue()


def _ffn_up_prompt(xn, w_gate, w_up, conv_w, conv_b, n_seq, tm=1024, tf=256):
    m, d = xn.shape
    f = w_gate.shape[1]
    n_blocks = m // tm
    blocks_per_seq = n_blocks // n_seq
    n_steps = (f // tf) * n_blocks + 1
    assert d % V7X_MXU_DIM == 0 and tm % (d // V7X_MXU_DIM) == 0
    vmem = 2 * (tm * d * 2 + 2 * d * tf * 4 + tm * tf * 2) + 4 * d * tf * 2 + (2 * tm + SUBLANES) * tf * 4
    vmem += 4 * tm * tf * 4
    cur = lambda t: jnp.minimum(t, n_steps - 2)
    prev = lambda t: jnp.maximum(t - 1, 0)
    return pl.pallas_call(
        functools.partial(_ffn_up_prompt_kernel, tm=tm, tf=tf, n_blocks=n_blocks, n_steps=n_steps,
                          blocks_per_seq=blocks_per_seq),
        out_shape=(jax.ShapeDtypeStruct((m, f), BF16),
                   jax.ShapeDtypeStruct((n_seq, SUBLANES, f), F32)),
        grid=(n_steps,),
        in_specs=[pl.BlockSpec((tm, d), lambda t: (cur(t) % n_blocks, 0)),
                  pl.BlockSpec((d, tf), lambda t: (0, cur(t) // n_blocks)),
                  pl.BlockSpec((d, tf), lambda t: (0, cur(t) // n_blocks)),
                  pl.BlockSpec((CONV_W, tf), lambda t: (0, prev(t) // n_blocks)),
                  pl.BlockSpec((1, tf), lambda t: (0, prev(t) // n_blocks))],
        out_specs=(pl.BlockSpec((tm, tf), lambda t: (prev(t) % n_blocks, prev(t) // n_blocks)),
                   pl.BlockSpec((1, SUBLANES, tf),
                                lambda t: ((prev(t) % n_blocks) // blocks_per_seq, 0, prev(t) // n_blocks))),
        scratch_shapes=[pltpu.VMEM((d, 2 * tf), BF16),
                        pltpu.VMEM((tm + SUBLANES, tf), F32), pltpu.VMEM((tm, tf), F32)],
        compiler_params=_params(("arbitrary",), vmem + 4 * MIB),
        name="ffn_up_prompt",
    )(xn, w_gate, w_up, conv_w, conv_b.reshape(1, f))


def _ffn_up_sample_kernel(xn_ref, wg_ref, wu_ref, cw_ref, cb_ref, b0_ref, b1_ref,
                          h_ref, t0_ref, t1_ref, u_s, up_s, h_s, *, n_seq, t_len):
    xn = xn_ref[...]
    u = jnp.dot(xn, wg_ref[...].astype(BF16), preferred_element_type=F32)
    up = jnp.dot(xn, wu_ref[...].astype(BF16), preferred_element_type=F32)
    for l in range(u_s.shape[0]):
        ls = slice(l * LANES, (l + 1) * LANES)
        u_s[l] = u[:, ls]
        up_s[l] = up[:, ls]
        full = [b0_ref[:, ls], b1_ref[:, ls]]
        full += [u_s[l, pl.ds(t, n_seq, stride=t_len), :] for t in range(t_len)]
        for t in range(t_len):
            c = cb_ref[:, ls] + ((full[t] * cw_ref[0:1, ls] + full[t + 1] * cw_ref[1:2, ls])
                                 + full[t + 2] * cw_ref[2:3, ls])
            h_s[l, pl.ds(t, n_seq, stride=t_len), :] = _silu(c) * up_s[l, pl.ds(t, n_seq, stride=t_len), :]
        h_ref[:, ls] = h_s[l].astype(h_ref.dtype)
        t0_ref[:, ls] = full[t_len]
        t1_ref[:, ls] = full[t_len + 1]


def _ffn_up_sample(xn, w_gate, w_up, conv_w, conv_b, conv_buf, n_seq, tf=256):
    m, d = xn.shape
    f = w_gate.shape[1]
    t_len = m // n_seq
    buf2d = conv_buf.reshape(n_seq, (CONV_W - 1) * f)
    vmem = 2 * (m * d * 2 + 2 * d * tf * 4 + m * tf * 2 + 4 * n_seq * tf * 4) + 2 * d * tf * 2 + 5 * m * tf * 4
    return pl.pallas_call(
        functools.partial(_ffn_up_sample_kernel, n_seq=n_seq, t_len=t_len),
        out_shape=(jax.ShapeDtypeStruct((m, f), BF16),
                   jax.ShapeDtypeStruct((n_seq, f), F32),
                   jax.ShapeDtypeStruct((n_seq, f), F32)),
        grid=(f // tf,),
        in_specs=[pl.BlockSpec((m, d), lambda j: (0, 0)),
                  pl.BlockSpec((d, tf), lambda j: (0, j)),
                  pl.BlockSpec((d, tf), lambda j: (0, j)),
                  pl.BlockSpec((CONV_W, tf), lambda j: (0, j)),
                  pl.BlockSpec((1, tf), lambda j: (0, j)),
                  pl.BlockSpec((n_seq, tf), lambda j: (0, j)),
                  pl.BlockSpec((n_seq, tf), lambda j: (0, f // tf + j))],
        out_specs=(pl.BlockSpec((m, tf), lambda j: (0, j)),
                   pl.BlockSpec((n_seq, tf), lambda j: (0, j)),
                   pl.BlockSpec((n_seq, tf), lambda j: (0, j))),
        scratch_shapes=[pltpu.VMEM((tf // LANES, m, LANES), F32)] * 3,
        compiler_params=_params(("parallel",), vmem + 4 * MIB),
        name="ffn_up_sample",
    )(xn, w_gate, w_up, conv_w, conv_b.reshape(1, f), buf2d, buf2d)


def _retention_kernel(lg_ref, q_ref, k_ref, v_ref, g_ref, cos_ref, sin_ref, gn_ref, *rest,
                      chunk, n_seg, heads, carry):
    if carry:
        o_ref, sout_ref = rest
    else:
        s0_ref, o_ref, sout_ref = rest
    rows = chunk * n_seg
    shift = int(math.log2(chunk))
    half = RET_DK // 2
    cos = cos_ref[...]
    sin = sin_ref[...]

    def rot(x):
        x1 = x[:, :half]
        x2 = x[:, half:]
        return jnp.concatenate([x1 * cos - x2 * sin, x1 * sin + x2 * cos], axis=-1)

    ri = lax.broadcasted_iota(jnp.int32, (rows, rows), 0)
    ci = lax.broadcasted_iota(jnp.int32, (rows, rows), 1)
    ok = ri >= ci
    if n_seg > 1:
        ok = jnp.logical_and(ok, lax.shift_right_logical(ri, shift) == lax.shift_right_logical(ci, shift))
    diff = jnp.where(ok, (ri - ci).astype(F32), 0.0)
    row_id = lax.broadcasted_iota(jnp.int32, (rows, RET_DK), 0)
    pos = (row_id & (chunk - 1)).astype(F32)
    seg = lax.shift_right_logical(row_id, shift)

    if carry:
        @pl.when(pl.program_id(2) == 0)
        def _():
            sout_ref[...] = jnp.zeros_like(sout_ref)

    for h in range(heads):
        cs = slice(h * RET_DK, (h + 1) * RET_DK)
        log_g = lg_ref[pl.program_id(1) * heads + h]
        dmat = jnp.where(ok, jnp.exp(diff * log_g), 0.0)
        q_dec = jnp.exp((pos + 1.0) * log_g)
        k_dec = jnp.exp((chunk - 1.0 - pos) * log_g)
        s_dec = jnp.exp(jnp.full((1, RET_DV), float(chunk), F32) * log_g)

        qb = rot(q_ref[:, cs]).astype(BF16)
        kr = rot(k_ref[:, cs]) * (RET_DK ** -0.5)
        vb = v_ref[:, cs].astype(BF16)
        scores = lax.dot_general(qb, kr.astype(BF16), _NT, preferred_element_type=F32) * dmat
        o = jnp.dot(scores.astype(BF16), vb, preferred_element_type=F32)
        kd = kr * k_dec
        if carry:
            s = sout_ref[0, h]
            o = o + jnp.dot(qb, s.astype(BF16), preferred_element_type=F32) * q_dec
            sout_ref[0, h] = s * s_dec + lax.dot_general(kd.astype(BF16), vb, _TN, preferred_element_type=F32)
        else:
            for b in range(n_seg):
                mine = seg == b
                s = s0_ref[b, h]
                o_b = jnp.dot(qb, s.astype(BF16), preferred_element_type=F32) * q_dec
                o = o + jnp.where(mine, o_b, 0.0)
                kd_b = jnp.where(mine, kd, 0.0).astype(BF16)
                sout_ref[b, h] = s * s_dec + lax.dot_general(kd_b, vb, _TN, preferred_element_type=F32)

        y = o * lax.rsqrt(jnp.mean(o * o, axis=-1, keepdims=True) + EPS) * gn_ref[:, cs]
        o_ref[:, cs] = (y * _silu(g_ref[:, cs])).astype(o_ref.dtype)


def _retention(proj, log_g, cos, sin, ret_norm_g, state, n_seq, chunk, n_seg, heads):
    m = proj.shape[0]
    rows = chunk * n_seg
    carry = state is None
    width = heads * RET_DK
    cq, ck, cv, cg = (c // width for c in (COL_QR, COL_KR, COL_VR, COL_GR))
    n_hg = RET_HEADS // heads
    if carry:
        n_chunks = m // n_seq // rows
        grid = (n_seq, n_hg, n_chunks)
        row_blk = lambda b, hg, c, lg: b * n_chunks + c
        tab_blk = lambda b, hg, c, lg: (c, 0)
        st_blk = lambda b, hg, c, lg: (b, hg, 0, 0)
        sem = ("parallel", "parallel", "arbitrary")
    else:
        grid = (m // rows, n_hg)
        row_blk = lambda i, hg, lg: i
        tab_blk = lambda i, hg, lg: (0, 0)
        st_blk = lambda i, hg, lg: (i, hg, 0, 0)
        sem = ("parallel", "parallel")
    st_shape = (n_seg, heads, RET_DK, RET_DV)

    def col(c0):
        return pl.BlockSpec((rows, width), lambda *a: (row_blk(*a), c0 + a[1]))

    in_specs = [col(cq), col(ck), col(cv), col(cg),
                pl.BlockSpec((rows, RET_DK // 2), tab_blk),
                pl.BlockSpec((rows, RET_DK // 2), tab_blk),
                pl.BlockSpec((1, width), lambda *a: (0, a[1]))]
    args = [proj, proj, proj, proj, cos, sin, ret_norm_g.reshape(1, RET_V)]
    if not carry:
        in_specs.append(pl.BlockSpec(st_shape, st_blk))
        args.append(state)
    st_bytes = n_seg * heads * RET_DK * RET_DV * 4
    vmem = 2 * (6 * rows * width * 4 + 2 * st_bytes) + 16 * rows * RET_DK * 4 + 4 * RET_DK * RET_DV * 4
    return pl.pallas_call(
        functools.partial(_retention_kernel, chunk=chunk, n_seg=n_seg, heads=heads, carry=carry),
        out_shape=(jax.ShapeDtypeStruct((m, RET_V), BF16),
                   jax.ShapeDtypeStruct((n_seq, RET_HEADS, RET_DK, RET_DV), F32)),
        grid_spec=pltpu.PrefetchScalarGridSpec(
            num_scalar_prefetch=1, grid=grid, in_specs=in_specs,
            out_specs=(pl.BlockSpec((rows, width), lambda *a: (row_blk(*a), a[1])),
                       pl.BlockSpec(st_shape, st_blk))),
        compiler_params=_params(sem, vmem + 8 * MIB),
        name="retention_prompt" if carry else "retention_sample",
    )(log_g, *args)


def _hg_gates(f_raw, i_raw, lb):
    f = lb + (1.0 - lb) * jax.nn.sigmoid(f_raw)
    return jnp.log(f), 1.0 - f, _silu(i_raw)


def _hg_lower_bound(lb_ref):
    logits = lb_ref[...]
    e = jnp.exp(logits - jnp.max(logits, axis=0, keepdims=True))
    return e[0:1, :] / jnp.sum(e, axis=0, keepdims=True)


def _hg_cumsum(log_f, pos, chunk):
    b = log_f
    sh = 1
    while sh < chunk:
        b = b + jnp.where(pos >= sh, pltpu.roll(b, sh, 0), 0.0)
        sh *= 2
    return b


def _hg_intra(q, k, v, b, pos, chunk):
    o = jnp.sum(q * k, axis=-1, keepdims=True) * v
    for d in range(1, chunk):
        valid = pos >= d
        decay = jnp.exp(jnp.where(valid, b - pltpu.roll(b, d, 0), -jnp.inf))
        score = jnp.sum(q * decay * pltpu.roll(k, d, 0), axis=-1, keepdims=True)
        o = o + score * pltpu.roll(v, d, 0)
    return o


def _hg_finish(o, g_raw, gn):
    y = o * lax.rsqrt(jnp.mean(o * o, axis=-1, keepdims=True) + EPS) * gn
    return y * _silu(g_raw)


def _hg_intra_scores(q, k, b2, chunk, lane, row):
    n_blk = chunk // SUBLANES
    qs = [q[i * SUBLANES:(i + 1) * SUBLANES] for i in range(n_blk)]
    bs = [b2[i * SUBLANES:(i + 1) * SUBLANES] for i in range(n_blk)]
    c2 = b2 - jnp.log2(jnp.maximum(k, 0.0))
    p = [jnp.zeros((SUBLANES, HG_DK), F32) for _ in range(n_blk)]
    for s in range(chunk):
        c_s = jnp.broadcast_to(c2[s:s + 1, :], (SUBLANES, HG_DK))
        for i in range(s // SUBLANES, n_blk):
            col = jnp.sum(qs[i] * jnp.exp2(bs[i] - c_s), axis=-1, keepdims=True)
            sel = lane == s
            if i == s // SUBLANES:
                sel = jnp.logical_and(sel, row >= (s - i * SUBLANES))
            p[i] = jnp.where(sel, col, p[i])
    return jnp.concatenate(p, axis=0)


def _hgrn_prompt_kernel(f_ref, q_ref, i_ref, g_ref, lb_ref, gn_ref, o_ref, sout_ref,
                        st_ref, p_s, qd_s, kd_s, v_s, e_s, *, heads, rows, chunk):
    tb = pl.program_id(2)

    @pl.when(tb == 0)
    def _():
        st_ref[...] = jnp.zeros_like(st_ref)

    lb = _hg_lower_bound(lb_ref)
    gn = gn_ref[...]
    pos = lax.broadcasted_iota(jnp.int32, (chunk, HG_DK), 0)
    lane = lax.broadcasted_iota(jnp.int32, (SUBLANES, HG_DK), 1)
    row = lax.broadcasted_iota(jnp.int32, (SUBLANES, HG_DK), 0)

    def front(c):
        rs = pl.ds(pl.multiple_of(c * chunk, chunk), chunk)
        for h in range(heads):
            cs = slice(h * HG_DK, (h + 1) * HG_DK)
            q = q_ref[rs, cs]
            f = lb[:, cs] + (1.0 - lb[:, cs]) * jax.nn.sigmoid(f_ref[rs, cs])
            k = 1.0 - f
            b2 = _hg_cumsum(jnp.log2(f), pos, chunk)
            p_s[h] = _hg_intra_scores(q, k, b2, chunk, lane, row).astype(BF16)
            b_last = b2[chunk - 1:chunk, :]
            qd_s[h] = (q * jnp.exp2(b2)).astype(BF16)
            kd_s[h] = (k * jnp.exp2(b_last - b2)).astype(BF16)
            v_s[h] = _silu(i_ref[rs, cs]).astype(BF16)
            e_s[h] = jnp.broadcast_to(jnp.exp2(b_last), (SUBLANES, HG_DK))

    def back(c):
        rs = pl.ds(pl.multiple_of(c * chunk, chunk), chunk)
        for h in range(heads):
            cs = slice(h * HG_DK, (h + 1) * HG_DK)
            vb = v_s[h]
            st = st_ref[h]
            o = jnp.dot(p_s[h][:, :chunk], vb, preferred_element_type=F32)
            o = o + lax.dot_general(qd_s[h], st.astype(BF16), _NT, preferred_element_type=F32)
            st_ref[h] = st * e_s[h][0:1, :] + lax.dot_general(vb, kd_s[h], _TN, preferred_element_type=F32)
            o_ref[rs, cs] = _hg_finish(o, g_ref[rs, cs], gn[:, cs]).astype(o_ref.dtype)

    def body(c, carry):
        back(c - 1)
        front(c)
        return carry

    n_chunks = rows // chunk
    front(0)
    lax.fori_loop(1, n_chunks, body, 0)
    back(n_chunks - 1)

    @pl.when(tb == pl.num_programs(2) - 1)
    def _():
        for h in range(heads):
            sout_ref[0, h] = st_ref[h].T


def _hgrn_prompt(proj, lb_logits, hg_norm_g, n_seq, chunk, heads=8, rows=1024):
    m = proj.shape[0]
    n_tb = m // n_seq // rows
    width = heads * HG_DK
    cf, cq, ci, cg = (c // width for c in (COL_FH, COL_QH, COL_IH, COL_GH))

    def col(c0):
        return pl.BlockSpec((rows, width), lambda b, hg, t: (b * n_tb + t, c0 + hg))

    vmem = 2 * (4 * rows * width * 4 + rows * width * 2 + heads * HG_DK * HG_DV * 4) + heads * HG_DK * HG_DV * 4
    return pl.pallas_call(
        functools.partial(_hgrn_prompt_kernel, heads=heads, rows=rows, chunk=chunk),
        out_shape=(jax.ShapeDtypeStruct((m, HG_V), BF16),
                   jax.ShapeDtypeStruct((n_seq, HG_HEADS, HG_DK, HG_DV), F32)),
        grid=(n_seq, HG_HEADS // heads, n_tb),
        in_specs=[col(cf), col(cq), col(ci), col(cg),
                  pl.BlockSpec((lb_logits.shape[0], width), lambda b, hg, t: (0, hg)),
                  pl.BlockSpec((1, width), lambda b, hg, t: (0, hg))],
        out_specs=(pl.BlockSpec((rows, width), lambda b, hg, t: (b * n_tb + t, hg)),
                   pl.BlockSpec((1, heads, HG_DK, HG_DV), lambda b, hg, t: (b, hg, 0, 0))),
        scratch_shapes=[pltpu.VMEM((heads, HG_DV, HG_DK), F32)]
                       + [pltpu.VMEM((heads, chunk, HG_DK), BF16)] * 4
                       + [pltpu.VMEM((heads, SUBLANES, HG_DK), F32)],
        compiler_params=_params(("parallel", "parallel", "arbitrary"), vmem + 8 * MIB),
        name="hgrn_prompt",
    )(proj, proj, proj, proj, lb_logits, hg_norm_g.reshape(1, HG_V))


def _hgrn_sample_kernel(f_ref, q_ref, i_ref, g_ref, lb_ref, gn_ref, s0_ref, o_ref, sout_ref,
                        *, heads, n_seg, chunk):
    rows = n_seg * chunk
    shift = int(math.log2(chunk))
    lb = _hg_lower_bound(lb_ref)
    gn = gn_ref[...]
    row_id = lax.broadcasted_iota(jnp.int32, (rows, HG_DK), 0)
    pos = row_id & (chunk - 1)
    seg = lax.shift_right_logical(row_id, shift)
    eye = (lax.broadcasted_iota(jnp.int32, (HG_DK, HG_DK), 0)
           == lax.broadcasted_iota(jnp.int32, (HG_DK, HG_DK), 1))
    for h in range(heads):
        cs = slice(h * HG_DK, (h + 1) * HG_DK)
        q = q_ref[:, cs]
        log_f, k, v = _hg_gates(f_ref[:, cs], i_ref[:, cs], lb[:, cs])
        b = _hg_cumsum(log_f, pos, chunk)
        o = _hg_intra(q, k, v, b, pos, chunk)
        b_last = b
        for j in range(1, chunk):
            b_last = jnp.where(pos == chunk - 1 - j, pltpu.roll(b, rows - j, 0), b_last)
        q_dec = (q * jnp.exp(b)).astype(BF16)
        k_dec = k * jnp.exp(b_last - b)
        vb = v.astype(BF16)
        for s in range(n_seg):
            mine = seg == s
            st = s0_ref[s, h]
            o = o + jnp.where(mine, jnp.dot(q_dec, st.astype(BF16), preferred_element_type=F32), 0.0)
            e_row = jnp.exp(b[(s + 1) * chunk - 1:(s + 1) * chunk, :])
            e_col = jnp.sum(jnp.where(eye, e_row, 0.0), axis=-1, keepdims=True)
            k_s = jnp.where(mine, k_dec, 0.0).astype(BF16)
            sout_ref[s, h] = e_col * st + lax.dot_general(k_s, vb, _TN, preferred_element_type=F32)
        o_ref[:, cs] = _hg_finish(o, g_ref[:, cs], gn[:, cs]).astype(o_ref.dtype)


def _hgrn_sample(proj, lb_logits, hg_norm_g, state, chunk, heads=8, n_seg=8):
    m = proj.shape[0]
    n_seq = m // chunk
    rows = n_seg * chunk
    width = heads * HG_DK
    cf, cq, ci, cg = (c // width for c in (COL_FH, COL_QH, COL_IH, COL_GH))

    def col(c0):
        return pl.BlockSpec((rows, width), lambda i, hg: (i, c0 + hg))

    st_spec = pl.BlockSpec((n_seg, heads, HG_DK, HG_DV), lambda i, hg: (i, hg, 0, 0))
    st_bytes = n_seg * heads * HG_DK * HG_DV * 4
    vmem = 2 * (5 * rows * width * 4 + 2 * st_bytes)
    return pl.pallas_call(
        functools.partial(_hgrn_sample_kernel, heads=heads, n_seg=n_seg, chunk=chunk),
        out_shape=(jax.ShapeDtypeStruct((m, HG_V), BF16),
                   jax.ShapeDtypeStruct((n_seq, HG_HEADS, HG_DK, HG_DV), F32)),
        grid=(m // rows, HG_HEADS // heads),
        in_specs=[col(cf), col(cq), col(ci), col(cg),
                  pl.BlockSpec((lb_logits.shape[0], width), lambda i, hg: (0, hg)),
                  pl.BlockSpec((1, width), lambda i, hg: (0, hg)),
                  st_spec],
        out_specs=(pl.BlockSpec((rows, width), lambda i, hg: (i, hg)), st_spec),
        compiler_params=_params(("parallel", "parallel"), vmem + 8 * MIB),
        name="hgrn_sample",
    )(proj, proj, proj, proj, lb_logits, hg_norm_g.reshape(1, HG_V), state)


def _rope_tables(pos):
    half = RET_DK // 2
    inv = ROPE_BASE ** (-jnp.arange(half, dtype=F32) / half)
    ang = pos[:, None] * inv[None, :]
    return jnp.cos(ang), jnp.sin(ang)


def _trunk(x, n_seq, pos, ret_state, hg_state, conv_buf, p, ret_seg, hg_seg):
    m = x.shape[0]
    t_len = m // n_seq
    prompt = ret_state is None
    tm = min(m, 1024)
    ret_chunk = math.gcd(t_len, RET_CHUNK)
    hg_chunk = math.gcd(t_len, HG_CHUNK)
    log_g = jnp.log1p(-jnp.exp2(-5.0 - jnp.arange(RET_HEADS, dtype=F32)))
    cos, sin = _rope_tables(pos)
    if not prompt:
        cos = jnp.tile(cos, (ret_seg, 1))
        sin = jnp.tile(sin, (ret_seg, 1))

    xn = _rmsnorm(x, p["norm_mix_g"], BF16)
    proj = _in_proj(xn, p["w_in"], tm, tn=512 if prompt else 1024)
    o_r, ret_new = _retention(proj, log_g, cos, sin, p["ret_norm_g"], ret_state, n_seq, ret_chunk,
                              1 if prompt else ret_seg, RET_HEADS if prompt else 2)
    if prompt:
        o_h, hg_new = _hgrn_prompt(proj, p["hg_lb_logits"], p["hg_norm_g"], n_seq, hg_chunk)
    else:
        o_h, hg_new = _hgrn_sample(proj, p["hg_lb_logits"], p["hg_norm_g"], hg_state, hg_chunk, n_seg=hg_seg)
    merged = _merge(o_r, o_h, p["w_br_ret"], p["w_br_hg"], proj, tm)
    x1 = _out_proj(merged, p["w_out"], x, tm)

    xn2 = _rmsnorm(x1, p["norm_ffn_g"], BF16)
    if prompt:
        h, tail = _ffn_up_prompt(xn2, p["w_gate"], p["w_up"], p["conv_w"], p["conv_b"], n_seq, tm=tm)
        conv_new = tail[:, SUBLANES - (CONV_W - 1):, :]
    else:
        h, t0, t1 = _ffn_up_sample(xn2, p["w_gate"], p["w_up"], p["conv_w"], p["conv_b"], conv_buf, n_seq)
        conv_new = jnp.stack([t0, t1], axis=1)
    down = _ffn_down(h, p["w_down"], tm)
    y = _rmsnorm(x1, p["final_norm_g"], F32, add=down)
    return y, ret_new, hg_new, conv_new


def kernel(x_prompt, x_sample, state_ret, state_hgrn, state_ffn_conv, norm_mix_g, w_in, ret_norm_g,
           hg_norm_g, hg_lb_logits, w_br_ret, w_br_hg, w_out, norm_ffn_g, w_gate, conv_w, conv_b,
           w_up, w_down, final_norm_g):
    assert norm_mix_g.shape[0] == 1, "single-layer trunk"
    bp, tp, d = x_prompt.shape
    bs, ts, _ = x_sample.shape
    p = dict(norm_mix_g=norm_mix_g[0], w_in=w_in[0], ret_norm_g=ret_norm_g[0], hg_norm_g=hg_norm_g[0],
             hg_lb_logits=hg_lb_logits, w_br_ret=w_br_ret[0], w_br_hg=w_br_hg[0], w_out=w_out[0],
             norm_ffn_g=norm_ffn_g[0], w_gate=w_gate[0], conv_w=conv_w[0], conv_b=conv_b[0],
             w_up=w_up[0], w_down=w_down[0], final_norm_g=final_norm_g)
    pos_p = jnp.arange(tp, dtype=F32)
    pos_s = PAST_LEN + jnp.arange(ts, dtype=F32)

    yp, rp, hp, cp = _trunk(x_prompt.reshape(bp * tp, d), bp, pos_p, None, None, None, p, 1, 1)
    ys, rs, hs, cs = _trunk(x_sample.reshape(bs * ts, d), bs, pos_s, state_ret[0], state_hgrn[0],
                            state_ffn_conv[0], p, 8, 8)
    return (yp.reshape(bp, tp, d), ys.reshape(bs, ts, d), rp[None], hp[None], cp[None],
            rs[None], hs[None], cs[None])
```

```python
import functools
import math

import jax
import jax.numpy as jnp
from jax import lax
from jax.experimental import pallas as pl
from jax.experimental.pallas import tpu as pltpu

F32 = jnp.float32
BF16 = jnp.bfloat16

D_MODEL = 4096
PAST_LEN = 16384
RET_HEADS = 8
RET_DK = 256
RET_DV = 256
RET_CHUNK = 128
ROPE_BASE = 10000.0
HG_HEADS = 16
HG_DK = 128
HG_DV = 128
HG_CHUNK = 16
D_FF = 11008
CONV_W = 3
EPS = 1e-6

RET_QK = RET_HEADS * RET_DK
RET_V = RET_HEADS * RET_DV
HG_K = HG_HEADS * HG_DK
HG_V = HG_HEADS * HG_DV
COL_QR = 0
COL_KR = COL_QR + RET_QK
COL_VR = COL_KR + RET_QK
COL_GR = COL_VR + RET_V
COL_FH = COL_GR + RET_V
COL_QH = COL_FH + HG_K
COL_IH = COL_QH + HG_K
COL_GH = COL_IH + HG_V
COL_GATE_RET = COL_GH + HG_V
COL_GATE_HG = COL_GATE_RET + D_MODEL
IN_COLS = COL_GATE_HG + D_MODEL

LANES = 128
SUBLANES = 8
V7X_MXU_DIM = 256
V7X_VMEM_BYTES = 64 * 1024 * 1024
MIB = 1024 * 1024

_NT = (((1,), (1,)), ((), ()))
_TN = (((0,), (0,)), ((), ()))


def _params(semantics, vmem_bytes):
    return pltpu.CompilerParams(dimension_semantics=semantics,
                                vmem_limit_bytes=min(int(vmem_bytes), V7X_VMEM_BYTES - 4 * MIB))


def _silu(x):
    return x * jax.nn.sigmoid(x)


def _rms_kernel(x_ref, g_ref, o_ref):
    x = x_ref[...]
    y = x * lax.rsqrt(jnp.mean(x * x, axis=-1, keepdims=True) + EPS)
    o_ref[...] = (y * g_ref[...]).astype(o_ref.dtype)


def _rms_add_kernel(x_ref, d_ref, g_ref, o_ref):
    x = x_ref[...] + d_ref[...]
    y = x * lax.rsqrt(jnp.mean(x * x, axis=-1, keepdims=True) + EPS)
    o_ref[...] = (y * g_ref[...]).astype(o_ref.dtype)


def _rmsnorm(x, g, out_dtype, add=None, tr=256):
    m, d = x.shape
    row = pl.BlockSpec((tr, d), lambda i: (i, 0))
    vec = pl.BlockSpec((1, d), lambda i: (0, 0))
    ins = [x] if add is None else [x, add]
    return pl.pallas_call(
        _rms_kernel if add is None else _rms_add_kernel,
        out_shape=jax.ShapeDtypeStruct((m, d), out_dtype),
        grid=(m // tr,),
        in_specs=[row] * len(ins) + [vec],
        out_specs=row,
        compiler_params=_params(("parallel",), 2 * (len(ins) + 2) * tr * d * 4),
        name="rmsnorm",
    )(*ins, g.reshape(1, d))


def _mm_kernel(x_ref, w_ref, o_ref):
    o_ref[...] = jnp.dot(x_ref[...], w_ref[...].astype(BF16), preferred_element_type=F32)


def _in_proj(xn, w, tm, tn=512):
    m, k = xn.shape
    n = w.shape[1]
    vmem = 2 * (tm * k * 2 + k * tn * 4 + tm * tn * 4) + k * tn * 2 + tm * tn * 4
    return pl.pallas_call(
        _mm_kernel,
        out_shape=jax.ShapeDtypeStruct((m, n), F32),
        grid=(m // tm, n // tn),
        in_specs=[pl.BlockSpec((tm, k), lambda i, j: (i, 0)),
                  pl.BlockSpec((k, tn), lambda i, j: (0, j))],
        out_specs=pl.BlockSpec((tm, tn), lambda i, j: (i, j)),
        compiler_params=_params(("parallel", "arbitrary"), vmem + 4 * MIB),
        name="in_proj",
    )(xn, w)


def _merge_kernel(or_ref, oh_ref, wr_ref, wh_ref, gr_ref, gh_ref, o_ref):
    pr = jnp.dot(or_ref[...], wr_ref[...].astype(BF16), preferred_element_type=F32)
    ph = jnp.dot(oh_ref[...], wh_ref[...].astype(BF16), preferred_element_type=F32)
    o_ref[...] = (jax.nn.sigmoid(gr_ref[...]) * pr + jax.nn.sigmoid(gh_ref[...]) * ph).astype(o_ref.dtype)


def _merge(o_r, o_h, w_br_ret, w_br_hg, proj, tm, tn=512):
    m = o_r.shape[0]
    kr, kh = o_r.shape[1], o_h.shape[1]
    n = D_MODEL
    vmem = 2 * (tm * (kr + kh) * 2 + (kr + kh) * tn * 4 + 2 * tm * tn * 4 + tm * tn * 2)
    vmem += (kr + kh) * tn * 2 + 3 * tm * tn * 4
    return pl.pallas_call(
        _merge_kernel,
        out_shape=jax.ShapeDtypeStruct((m, n), BF16),
        grid=(m // tm, n // tn),
        in_specs=[pl.BlockSpec((tm, kr), lambda i, j: (i, 0)),
                  pl.BlockSpec((tm, kh), lambda i, j: (i, 0)),
                  pl.BlockSpec((kr, tn), lambda i, j: (0, j)),
                  pl.BlockSpec((kh, tn), lambda i, j: (0, j)),
                  pl.BlockSpec((tm, tn), lambda i, j: (i, COL_GATE_RET // tn + j)),
                  pl.BlockSpec((tm, tn), lambda i, j: (i, COL_GATE_HG // tn + j))],
        out_specs=pl.BlockSpec((tm, tn), lambda i, j: (i, j)),
        compiler_params=_params(("parallel", "arbitrary"), vmem + 4 * MIB),
        name="merge",
    )(o_r, o_h, w_br_ret, w_br_hg, proj, proj)


def _out_proj_kernel(m_ref, w_ref, x_ref, o_ref):
    o_ref[...] = x_ref[...] + jnp.dot(m_ref[...], w_ref[...].astype(BF16), preferred_element_type=F32)


def _out_proj(merged, w, x, tm, tn=512):
    m, k = merged.shape
    n = w.shape[1]
    vmem = 2 * (tm * k * 2 + k * tn * 4 + 2 * tm * tn * 4) + k * tn * 2 + tm * tn * 4
    return pl.pallas_call(
        _out_proj_kernel,
        out_shape=jax.ShapeDtypeStruct((m, n), F32),
        grid=(m // tm, n // tn),
        in_specs=[pl.BlockSpec((tm, k), lambda i, j: (i, 0)),
                  pl.BlockSpec((k, tn), lambda i, j: (0, j)),
                  pl.BlockSpec((tm, tn), lambda i, j: (i, j))],
        out_specs=pl.BlockSpec((tm, tn), lambda i, j: (i, j)),
        compiler_params=_params(("parallel", "arbitrary"), vmem + 4 * MIB),
        name="out_proj",
    )(merged, w, x)


def _ffn_down_kernel(h_ref, w_ref, o_ref, *, k_last):
    k = pl.program_id(1)
    n_k = pl.num_programs(1)

    @pl.when(k == 0)
    def _():
        o_ref[...] = jnp.dot(h_ref[...], w_ref[...].astype(BF16), preferred_element_type=F32)

    @pl.when(jnp.logical_and(k > 0, k < n_k - 1))
    def _():
        o_ref[...] += jnp.dot(h_ref[...], w_ref[...].astype(BF16), preferred_element_type=F32)

    @pl.when(k == n_k - 1)
    def _():
        o_ref[...] += jnp.dot(h_ref[:, :k_last], w_ref[:k_last, :].astype(BF16), preferred_element_type=F32)


def _ffn_down(h, w, tm, tk=512):
    m, f = h.shape
    n = w.shape[1]
    n_k = pl.cdiv(f, tk)
    assert n_k >= 2
    vmem = 2 * (tm * tk * 2 + tk * n * 4 + tm * n * 4) + tk * n * 2 + tm * n * 4
    return pl.pallas_call(
        functools.partial(_ffn_down_kernel, k_last=f - (n_k - 1) * tk),
        out_shape=jax.ShapeDtypeStruct((m, n), F32),
        grid=(m // tm, n_k),
        in_specs=[pl.BlockSpec((tm, tk), lambda i, k: (i, k)),
                  pl.BlockSpec((tk, n), lambda i, k: (k, 0))],
        out_specs=pl.BlockSpec((tm, n), lambda i, k: (i, 0)),
        compiler_params=_params(("parallel", "arbitrary"), vmem + 4 * MIB),
        name="ffn_down",
    )(h, w)


def _conv_gate(full0, full1, full2, cw_ref, cb_ref, up):
    c = cb_ref[...] + ((full0 * cw_ref[0:1, :] + full1 * cw_ref[1:2, :]) + full2 * cw_ref[2:3, :])
    return _silu(c) * up


def _zero_after(x):
    r = jnp.max(x, axis=0, keepdims=True).astype(F32)
    bits = pltpu.bitcast(r, jnp.uint32)
    bits = lax.shift_right_logical(lax.shift_right_logical(bits, jnp.uint32(16)), jnp.uint32(16))
    return pltpu.bitcast(bits, F32).astype(BF16)


def _ffn_up_prompt_kernel(xn_ref, wg_ref, wu_ref, cw_ref, cb_ref, h_ref, tail_ref,
                          w_s, u_s, up_s, *, tm, tf, n_blocks, n_steps, blocks_per_seq):
    t = pl.program_id(0)
    i = lax.rem(t, n_blocks)
    n_kt = xn_ref.shape[1] // V7X_MXU_DIM
    rc = tm // n_kt

    def store_result(res):
        u_s[SUBLANES:SUBLANES + tm, :] = res[:, :tf]
        up_s[...] = res[:, tf:]

    def epilogue_rows(r0, nrows):
        h = _conv_gate(u_s[SUBLANES - 2 + r0:SUBLANES - 2 + r0 + nrows, :],
                       u_s[SUBLANES - 1 + r0:SUBLANES - 1 + r0 + nrows, :],
                       u_s[SUBLANES + r0:SUBLANES + r0 + nrows, :], cw_ref, cb_ref,
                       up_s[r0:r0 + nrows, :]).astype(h_ref.dtype)
        h_ref[r0:r0 + nrows, :] = h
        return h

    def finish_epilogue():
        last = u_s[tm:tm + SUBLANES, :]
        tail_ref[0] = last
        return last

    def cast_weights():
        w_s[:, :tf] = wg_ref[...].astype(BF16)
        w_s[:, tf:] = wu_ref[...].astype(BF16)

    @pl.when(t == 0)
    def _():
        cast_weights()
        u_s[0:SUBLANES, :] = jnp.zeros((SUBLANES, u_s.shape[1]), F32)
        store_result(jnp.dot(xn_ref[...], w_s[...], preferred_element_type=F32))

    def steady(first_of_tile):
        if first_of_tile:
            cast_weights()
        pieces = []
        for kt in range(n_kt):
            zero = _zero_after(epilogue_rows(kt * rc, rc))
            w_kt = w_s[kt * V7X_MXU_DIM:(kt + 1) * V7X_MXU_DIM, :]
            pieces.append(w_kt + jnp.concatenate([zero, zero], axis=1))
        last = finish_epilogue()
        res = jnp.dot(xn_ref[...], jnp.concatenate(pieces, axis=0), preferred_element_type=F32)
        u_s[0:SUBLANES, :] = jnp.where(lax.rem(i, blocks_per_seq) == 0, 0.0, last)
        store_result(res)

    @pl.when(jnp.logical_and(t > 0, jnp.logical_and(t < n_steps - 1, i == 0)))
    def _():
        steady(True)

    @pl.when(jnp.logical_and(t < n_steps - 1, i > 0))
    def _():
        steady(False)

    @pl.when(t == n_steps - 1)
    def _():
        epilogue_rows(0, tm)
        finish_epilogue()


def _ffn_up_prompt(xn, w_gate, w_up, conv_w, conv_b, n_seq, tm=1024, tf=256):
    m, d = xn.shape
    f = w_gate.shape[1]
    n_blocks = m // tm
    blocks_per_seq = n_blocks // n_seq
    n_steps = (f // tf) * n_blocks + 1
    assert d % V7X_MXU_DIM == 0 and tm % (d // V7X_MXU_DIM) == 0 and n_blocks >= 2
    vmem = 2 * (tm * d * 2 + 2 * d * tf * 4 + tm * tf * 2) + 4 * d * tf * 2 + (2 * tm + SUBLANES) * tf * 4
    vmem += 4 * tm * tf * 4
    cur = lambda t: jnp.minimum(t, n_steps - 2)
    prev = lambda t: jnp.maximum(t - 1, 0)
    up_tile = lambda t: jnp.minimum(cur(t) + 1, n_steps - 2) // n_blocks
    return pl.pallas_call(
        functools.partial(_ffn_up_prompt_kernel, tm=tm, tf=tf, n_blocks=n_blocks, n_steps=n_steps,
                          blocks_per_seq=blocks_per_seq),
        out_shape=(jax.ShapeDtypeStruct((m, f), BF16),
                   jax.ShapeDtypeStruct((n_seq, SUBLANES, f), F32)),
        grid=(n_steps,),
        in_specs=[pl.BlockSpec((tm, d), lambda t: (cur(t) % n_blocks, 0)),
                  pl.BlockSpec((d, tf), lambda t: (0, cur(t) // n_blocks)),
                  pl.BlockSpec((d, tf), lambda t: (0, up_tile(t))),
                  pl.BlockSpec((CONV_W, tf), lambda t: (0, prev(t) // n_blocks)),
                  pl.BlockSpec((1, tf), lambda t: (0, prev(t) // n_blocks))],
        out_specs=(pl.BlockSpec((tm, tf), lambda t: (prev(t) % n_blocks, prev(t) // n_blocks)),
                   pl.BlockSpec((1, SUBLANES, tf),
                                lambda t: ((prev(t) % n_blocks) // blocks_per_seq, 0, prev(t) // n_blocks))),
        scratch_shapes=[pltpu.VMEM((d, 2 * tf), BF16),
                        pltpu.VMEM((tm + SUBLANES, tf), F32), pltpu.VMEM((tm, tf), F32)],
        compiler_params=_params(("arbitrary",), vmem + 4 * MIB),
        name="ffn_up_prompt",
    )(xn, w_gate, w_up, conv_w, conv_b.reshape(1, f))


def _ffn_up_sample_kernel(xn_ref, wg_ref, wu_ref, cw_ref, cb_ref, b0_ref, b1_ref,
                          h_ref, t0_ref, t1_ref, u_s, up_s, h_s, *, n_seq, t_len):
    xn = xn_ref[...]
    u = jnp.dot(xn, wg_ref[...].astype(BF16), preferred_element_type=F32)
    up = jnp.dot(xn, wu_ref[...].astype(BF16), preferred_element_type=F32)
    for l in range(u_s.shape[0]):
        ls = slice(l * LANES, (l + 1) * LANES)
        u_s[l] = u[:, ls]
        up_s[l] = up[:, ls]
        full = [b0_ref[:, ls], b1_ref[:, ls]]
        full += [u_s[l, pl.ds(t, n_seq, stride=t_len), :] for t in range(t_len)]
        for t in range(t_len):
            c = cb_ref[:, ls] + ((full[t] * cw_ref[0:1, ls] + full[t + 1] * cw_ref[1:2, ls])
                                 + full[t + 2] * cw_ref[2:3, ls])
            h_s[l, pl.ds(t, n_seq, stride=t_len), :] = _silu(c) * up_s[l, pl.ds(t, n_seq, stride=t_len), :]
        h_ref[:, ls] = h_s[l].astype(h_ref.dtype)
        t0_ref[:, ls] = full[t_len]
        t1_ref[:, ls] = full[t_len + 1]


def _ffn_up_sample(xn, w_gate, w_up, conv_w, conv_b, conv_buf, n_seq, tf=256):
    m, d = xn.shape
    f = w_gate.shape[1]
    t_len = m // n_seq
    buf2d = conv_buf.reshape(n_seq, (CONV_W - 1) * f)
    vmem = 2 * (m * d * 2 + 2 * d * tf * 4 + m * tf * 2 + 4 * n_seq * tf * 4) + 2 * d * tf * 2 + 5 * m * tf * 4
    return pl.pallas_call(
        functools.partial(_ffn_up_sample_kernel, n_seq=n_seq, t_len=t_len),
        out_shape=(jax.ShapeDtypeStruct((m, f), BF16),
                   jax.ShapeDtypeStruct((n_seq, f), F32),
                   jax.ShapeDtypeStruct((n_seq, f), F32)),
        grid=(f // tf,),
        in_specs=[pl.BlockSpec((m, d), lambda j: (0, 0)),
                  pl.BlockSpec((d, tf), lambda j: (0, j)),
                  pl.BlockSpec((d, tf), lambda j: (0, j)),
                  pl.BlockSpec((CONV_W, tf), lambda j: (0, j)),
                  pl.BlockSpec((1, tf), lambda j: (0, j)),
                  pl.BlockSpec((n_seq, tf), lambda j: (0, j)),
                  pl.BlockSpec((n_seq, tf), lambda j: (0, f // tf + j))],
        out_specs=(pl.BlockSpec((m, tf), lambda j: (0, j)),
                   pl.BlockSpec((n_seq, tf), lambda j: (0, j)),
                   pl.BlockSpec((n_seq, tf), lambda j: (0, j))),
        scratch_shapes=[pltpu.VMEM((tf // LANES, m, LANES), F32)] * 3,
        compiler_params=_params(("parallel",), vmem + 4 * MIB),
        name="ffn_up_sample",
    )(xn, w_gate, w_up, conv_w, conv_b.reshape(1, f), buf2d, buf2d)


def _retention_kernel(lg_ref, q_ref, k_ref, v_ref, g_ref, cos_ref, sin_ref, gn_ref, *rest,
                      chunk, n_seg, heads, carry):
    if carry:
        o_ref, sout_ref = rest
    else:
        s0_ref, o_ref, sout_ref = rest
    rows = chunk * n_seg
    shift = int(math.log2(chunk))
    half = RET_DK // 2
    cos = cos_ref[...]
    sin = sin_ref[...]

    def rot(x):
        x1 = x[:, :half]
        x2 = x[:, half:]
        return jnp.concatenate([x1 * cos - x2 * sin, x1 * sin + x2 * cos], axis=-1)

    ri = lax.broadcasted_iota(jnp.int32, (rows, rows), 0)
    ci = lax.broadcasted_iota(jnp.int32, (rows, rows), 1)
    ok = ri >= ci
    if n_seg > 1:
        ok = jnp.logical_and(ok, lax.shift_right_logical(ri, shift) == lax.shift_right_logical(ci, shift))
    diff = jnp.where(ok, (ri - ci).astype(F32), 0.0)
    row_id = lax.broadcasted_iota(jnp.int32, (rows, RET_DK), 0)
    pos = (row_id & (chunk - 1)).astype(F32)
    seg = lax.shift_right_logical(row_id, shift)

    if carry:
        @pl.when(pl.program_id(2) == 0)
        def _():
            sout_ref[...] = jnp.zeros_like(sout_ref)

    for h in range(heads):
        cs = slice(h * RET_DK, (h + 1) * RET_DK)
        log_g = lg_ref[pl.program_id(1) * heads + h]
        dmat = jnp.where(ok, jnp.exp(diff * log_g), 0.0)
        q_dec = jnp.exp((pos + 1.0) * log_g)
        k_dec = jnp.exp((chunk - 1.0 - pos) * log_g)
        s_dec = jnp.exp(jnp.full((1, RET_DV), float(chunk), F32) * log_g)

        qb = rot(q_ref[:, cs]).astype(BF16)
        kr = rot(k_ref[:, cs]) * (RET_DK ** -0.5)
        vb = v_ref[:, cs].astype(BF16)
        scores = lax.dot_general(qb, kr.astype(BF16), _NT, preferred_element_type=F32) * dmat
        o = jnp.dot(scores.astype(BF16), vb, preferred_element_type=F32)
        kd = kr * k_dec
        if carry:
            s = sout_ref[0, h]
            o = o + jnp.dot(qb, s.astype(BF16), preferred_element_type=F32) * q_dec
            sout_ref[0, h] = s * s_dec + lax.dot_general(kd.astype(BF16), vb, _TN, preferred_element_type=F32)
        else:
            for b in range(n_seg):
                mine = seg == b
                s = s0_ref[b, h]
                o_b = jnp.dot(qb, s.astype(BF16), preferred_element_type=F32) * q_dec
                o = o + jnp.where(mine, o_b, 0.0)
                kd_b = jnp.where(mine, kd, 0.0).astype(BF16)
                sout_ref[b, h] = s * s_dec + lax.dot_general(kd_b, vb, _TN, preferred_element_type=F32)

        y = o * lax.rsqrt(jnp.mean(o * o, axis=-1, keepdims=True) + EPS) * gn_ref[:, cs]
        o_ref[:, cs] = (y * _silu(g_ref[:, cs])).astype(o_ref.dtype)


def _retention(proj, log_g, cos, sin, ret_norm_g, state, n_seq, chunk, n_seg, heads):
    m = proj.shape[0]
    rows = chunk * n_seg
    carry = state is None
    width = heads * RET_DK
    cq, ck, cv, cg = (c // width for c in (COL_QR, COL_KR, COL_VR, COL_GR))
    n_hg = RET_HEADS // heads
    if carry:
        n_chunks = m // n_seq // rows
        grid = (n_seq, n_hg, n_chunks)
        row_blk = lambda b, hg, c, lg: b * n_chunks + c
        tab_blk = lambda b, hg, c, lg: (c, 0)
        st_blk = lambda b, hg, c, lg: (b, hg, 0, 0)
        sem = ("parallel", "parallel", "arbitrary")
    else:
        grid = (m // rows, n_hg)
        row_blk = lambda i, hg, lg: i
        tab_blk = lambda i, hg, lg: (0, 0)
        st_blk = lambda i, hg, lg: (i, hg, 0, 0)
        sem = ("parallel", "parallel")
    st_shape = (n_seg, heads, RET_DK, RET_DV)

    def col(c0):
        return pl.BlockSpec((rows, width), lambda *a: (row_blk(*a), c0 + a[1]))

    in_specs = [col(cq), col(ck), col(cv), col(cg),
                pl.BlockSpec((rows, RET_DK // 2), tab_blk),
                pl.BlockSpec((rows, RET_DK // 2), tab_blk),
                pl.BlockSpec((1, width), lambda *a: (0, a[1]))]
    args = [proj, proj, proj, proj, cos, sin, ret_norm_g.reshape(1, RET_V)]
    if not carry:
        in_specs.append(pl.BlockSpec(st_shape, st_blk))
        args.append(state)
    st_bytes = n_seg * heads * RET_DK * RET_DV * 4
    vmem = 2 * (6 * rows * width * 4 + 2 * st_bytes) + 16 * rows * RET_DK * 4 + 4 * RET_DK * RET_DV * 4
    return pl.pallas_call(
        functools.partial(_retention_kernel, chunk=chunk, n_seg=n_seg, heads=heads, carry=carry),
        out_shape=(jax.ShapeDtypeStruct((m, RET_V), BF16),
                   jax.ShapeDtypeStruct((n_seq, RET_HEADS, RET_DK, RET_DV), F32)),
        grid_spec=pltpu.PrefetchScalarGridSpec(
            num_scalar_prefetch=1, grid=grid, in_specs=in_specs,
            out_specs=(pl.BlockSpec((rows, width), lambda *a: (row_blk(*a), a[1])),
                       pl.BlockSpec(st_shape, st_blk))),
        compiler_params=_params(sem, vmem + 8 * MIB),
        name="retention_prompt" if carry else "retention_sample",
    )(log_g, *args)


def _hg_gates(f_raw, i_raw, lb):
    f = lb + (1.0 - lb) * jax.nn.sigmoid(f_raw)
    return jnp.log(f), 1.0 - f, _silu(i_raw)


def _hg_lower_bound(lb_ref):
    logits = lb_ref[...]
    e = jnp.exp(logits - jnp.max(logits, axis=0, keepdims=True))
    return e[0:1, :] / jnp.sum(e, axis=0, keepdims=True)


def _hg_cumsum(log_f, pos, chunk):
    b = log_f
    sh = 1
    while sh < chunk:
        b = b + jnp.where(pos >= sh, pltpu.roll(b, sh, 0), 0.0)
        sh *= 2
    return b


def _hg_intra(q, k, v, b, pos, chunk):
    o = jnp.sum(q * k, axis=-1, keepdims=True) * v
    for d in range(1, chunk):
        valid = pos >= d
        decay = jnp.exp(jnp.where(valid, b - pltpu.roll(b, d, 0), -jnp.inf))
        score = jnp.sum(q * decay * pltpu.roll(k, d, 0), axis=-1, keepdims=True)
        o = o + score * pltpu.roll(v, d, 0)
    return o


def _hg_finish(o, g_raw, gn):
    y = o * lax.rsqrt(jnp.mean(o * o, axis=-1, keepdims=True) + EPS) * gn
    return y * _silu(g_raw)


def _hg_intra_scores(q, k, b2, chunk, lane, row):
    n_blk = chunk // SUBLANES
    qs = [q[i * SUBLANES:(i + 1) * SUBLANES] for i in range(n_blk)]
    bs = [b2[i * SUBLANES:(i + 1) * SUBLANES] for i in range(n_blk)]
    c2 = b2 - jnp.log2(jnp.maximum(k, 0.0))
    p = [jnp.zeros((SUBLANES, HG_DK), F32) for _ in range(n_blk)]
    for s in range(chunk):
        c_s = jnp.broadcast_to(c2[s:s + 1, :], (SUBLANES, HG_DK))
        for i in range(s // SUBLANES, n_blk):
            col = jnp.sum(qs[i] * jnp.exp2(bs[i] - c_s), axis=-1, keepdims=True)
            sel = lane == s
            if i == s // SUBLANES:
                sel = jnp.logical_and(sel, row >= (s - i * SUBLANES))
            p[i] = jnp.where(sel, col, p[i])
    return jnp.concatenate(p, axis=0)


def _hgrn_prompt_kernel(f_ref, q_ref, i_ref, g_ref, lb_ref, gn_ref, o_ref, sout_ref,
                        st_ref, p_s, qd_s, kd_s, v_s, e_s, *, heads, rows, chunk):
    tb = pl.program_id(2)

    @pl.when(tb == 0)
    def _():
        st_ref[...] = jnp.zeros_like(st_ref)

    lb = _hg_lower_bound(lb_ref)
    gn = gn_ref[...]
    pos = lax.broadcasted_iota(jnp.int32, (chunk, HG_DK), 0)
    lane = lax.broadcasted_iota(jnp.int32, (SUBLANES, HG_DK), 1)
    row = lax.broadcasted_iota(jnp.int32, (SUBLANES, HG_DK), 0)

    def front(c):
        rs = pl.ds(pl.multiple_of(c * chunk, chunk), chunk)
        for h in range(heads):
            cs = slice(h * HG_DK, (h + 1) * HG_DK)
            q = q_ref[rs, cs]
            f = lb[:, cs] + (1.0 - lb[:, cs]) * jax.nn.sigmoid(f_ref[rs, cs])
            k = 1.0 - f
            b2 = _hg_cumsum(jnp.log2(f), pos, chunk)
            p_s[h] = _hg_intra_scores(q, k, b2, chunk, lane, row).astype(BF16)
            b_last = b2[chunk - 1:chunk, :]
            qd_s[h] = (q * jnp.exp2(b2)).astype(BF16)
            kd_s[h] = (k * jnp.exp2(b_last - b2)).astype(BF16)
            v_s[h] = _silu(i_ref[rs, cs]).astype(BF16)
            e_s[h] = jnp.broadcast_to(jnp.exp2(b_last), (SUBLANES, HG_DK))

    def back(c):
        rs = pl.ds(pl.multiple_of(c * chunk, chunk), chunk)
        for h in range(heads):
            cs = slice(h * HG_DK, (h + 1) * HG_DK)
            vb = v_s[h]
            st = st_ref[h]
            o = jnp.dot(p_s[h][:, :chunk], vb, preferred_element_type=F32)
            o = o + lax.dot_general(qd_s[h], st.astype(BF16), _NT, preferred_element_type=F32)
            st_ref[h] = st * e_s[h][0:1, :] + lax.dot_general(vb, kd_s[h], _TN, preferred_element_type=F32)
            o_ref[rs, cs] = _hg_finish(o, g_ref[rs, cs], gn[:, cs]).astype(o_ref.dtype)

    def body(c, carry):
        back(c - 1)
        front(c)
        return carry

    n_chunks = rows // chunk
    front(0)
    lax.fori_loop(1, n_chunks, body, 0)
    back(n_chunks - 1)

    @pl.when(tb == pl.num_programs(2) - 1)
    def _():
        for h in range(heads):
            sout_ref[0, h] = st_ref[h].T


def _hgrn_prompt(proj, lb_logits, hg_norm_g, n_seq, chunk, heads=16, rows=512):
    m = proj.shape[0]
    n_tb = m // n_seq // rows
    width = heads * HG_DK
    cf, cq, ci, cg = (c // width for c in (COL_FH, COL_QH, COL_IH, COL_GH))

    def col(c0):
        return pl.BlockSpec((rows, width), lambda b, hg, t: (b * n_tb + t, c0 + hg))

    vmem = 2 * (4 * rows * width * 4 + rows * width * 2 + heads * HG_DK * HG_DV * 4) + heads * HG_DK * HG_DV * 4
    return pl.pallas_call(
        functools.partial(_hgrn_prompt_kernel, heads=heads, rows=rows, chunk=chunk),
        out_shape=(jax.ShapeDtypeStruct((m, HG_V), BF16),
                   jax.ShapeDtypeStruct((n_seq, HG_HEADS, HG_DK, HG_DV), F32)),
        grid=(n_seq, HG_HEADS // heads, n_tb),
        in_specs=[col(cf), col(cq), col(ci), col(cg),
                  pl.BlockSpec((lb_logits.shape[0], width), lambda b, hg, t: (0, hg)),
                  pl.BlockSpec((1, width), lambda b, hg, t: (0, hg))],
        out_specs=(pl.BlockSpec((rows, width), lambda b, hg, t: (b * n_tb + t, hg)),
                   pl.BlockSpec((1, heads, HG_DK, HG_DV), lambda b, hg, t: (b, hg, 0, 0))),
        scratch_shapes=[pltpu.VMEM((heads, HG_DV, HG_DK), F32)]
                       + [pltpu.VMEM((heads, chunk, HG_DK), BF16)] * 4
                       + [pltpu.VMEM((heads, SUBLANES, HG_DK), F32)],
        compiler_params=_params(("parallel", "parallel", "arbitrary"), vmem + 8 * MIB),
        name="hgrn_prompt",
    )(proj, proj, proj, proj, lb_logits, hg_norm_g.reshape(1, HG_V))


def _hgrn_sample_kernel(f_ref, q_ref, i_ref, g_ref, lb_ref, gn_ref, s0_ref, o_ref, sout_ref,
                        *, heads, n_seg, chunk):
    rows = n_seg * chunk
    shift = int(math.log2(chunk))
    lb = _hg_lower_bound(lb_ref)
    gn = gn_ref[...]
    row_id = lax.broadcasted_iota(jnp.int32, (rows, HG_DK), 0)
    pos = row_id & (chunk - 1)
    seg = lax.shift_right_logical(row_id, shift)
    eye = (lax.broadcasted_iota(jnp.int32, (HG_DK, HG_DK), 0)
           == lax.broadcasted_iota(jnp.int32, (HG_DK, HG_DK), 1))
    for h in range(heads):
        cs = slice(h * HG_DK, (h + 1) * HG_DK)
        q = q_ref[:, cs]
        log_f, k, v = _hg_gates(f_ref[:, cs], i_ref[:, cs], lb[:, cs])
        b = _hg_cumsum(log_f, pos, chunk)
        o = _hg_intra(q, k, v, b, pos, chunk)
        b_last = b
        for j in range(1, chunk):
            b_last = jnp.where(pos == chunk - 1 - j, pltpu.roll(b, rows - j, 0), b_last)
        q_dec = (q * jnp.exp(b)).astype(BF16)
        k_dec = k * jnp.exp(b_last - b)
        vb = v.astype(BF16)
        for s in range(n_seg):
            mine = seg == s
            st = s0_ref[s, h]
            o = o + jnp.where(mine, jnp.dot(q_dec, st.astype(BF16), preferred_element_type=F32), 0.0)
            e_row = jnp.exp(b[(s + 1) * chunk - 1:(s + 1) * chunk, :])
            e_col = jnp.sum(jnp.where(eye, e_row, 0.0), axis=-1, keepdims=True)
            k_s = jnp.where(mine, k_dec, 0.0).astype(BF16)
            sout_ref[s, h] = e_col * st + lax.dot_general(k_s, vb, _TN, preferred_element_type=F32)
        o_ref[:, cs] = _hg_finish(o, g_ref[:, cs], gn[:, cs]).astype(o_ref.dtype)


def _hgrn_sample(proj, lb_logits, hg_norm_g, state, chunk, heads=8, n_seg=8):
    m = proj.shape[0]
    n_seq = m // chunk
    rows = n_seg * chunk
    width = heads * HG_DK
    cf, cq, ci, cg = (c // width for c in (COL_FH, COL_QH, COL_IH, COL_GH))

    def col(c0):
        return pl.BlockSpec((rows, width), lambda i, hg: (i, c0 + hg))

    st_spec = pl.BlockSpec((n_seg, heads, HG_DK, HG_DV), lambda i, hg: (i, hg, 0, 0))
    st_bytes = n_seg * heads * HG_DK * HG_DV * 4
    vmem = 2 * (5 * rows * width * 4 + 2 * st_bytes)
    return pl.pallas_call(
        functools.partial(_hgrn_sample_kernel, heads=heads, n_seg=n_seg, chunk=chunk),
        out_shape=(jax.ShapeDtypeStruct((m, HG_V), BF16),
                   jax.ShapeDtypeStruct((n_seq, HG_HEADS, HG_DK, HG_DV), F32)),
        grid=(m // rows, HG_HEADS // heads),
        in_specs=[col(cf), col(cq), col(ci), col(cg),
                  pl.BlockSpec((lb_logits.shape[0], width), lambda i, hg: (0, hg)),
                  pl.BlockSpec((1, width), lambda i, hg: (0, hg)),
                  st_spec],
        out_specs=(pl.BlockSpec((rows, width), lambda i, hg: (i, hg)), st_spec),
        compiler_params=_params(("parallel", "parallel"), vmem + 8 * MIB),
        name="hgrn_sample",
    )(proj, proj, proj, proj, lb_logits, hg_norm_g.reshape(1, HG_V), state)


def _rope_tables(pos):
    half = RET_DK // 2
    inv = ROPE_BASE ** (-jnp.arange(half, dtype=F32) / half)
    ang = pos[:, None] * inv[None, :]
    return jnp.cos(ang), jnp.sin(ang)


def _trunk(x, n_seq, pos, ret_state, hg_state, conv_buf, p, ret_seg, hg_seg):
    m = x.shape[0]
    t_len = m // n_seq
    prompt = ret_state is None
    tm = min(m, 1024)
    ret_chunk = math.gcd(t_len, RET_CHUNK)
    hg_chunk = math.gcd(t_len, HG_CHUNK)
    log_g = jnp.log1p(-jnp.exp2(-5.0 - jnp.arange(RET_HEADS, dtype=F32)))
    cos, sin = _rope_tables(pos)
    if not prompt:
        cos = jnp.tile(cos, (ret_seg, 1))
        sin = jnp.tile(sin, (ret_seg, 1))

    xn = _rmsnorm(x, p["norm_mix_g"], BF16)
    proj = _in_proj(xn, p["w_in"], tm, tn=512 if prompt else 1024)
    o_r, ret_new = _retention(proj, log_g, cos, sin, p["ret_norm_g"], ret_state, n_seq, ret_chunk,
                              1 if prompt else ret_seg, RET_HEADS if prompt else 2)
    if prompt:
        o_h, hg_new = _hgrn_prompt(proj, p["hg_lb_logits"], p["hg_norm_g"], n_seq, hg_chunk)
    else:
        o_h, hg_new = _hgrn_sample(proj, p["hg_lb_logits"], p["hg_norm_g"], hg_state, hg_chunk, n_seg=hg_seg)
    merged = _merge(o_r, o_h, p["w_br_ret"], p["w_br_hg"], proj, tm)
    x1 = _out_proj(merged, p["w_out"], x, tm)

    xn2 = _rmsnorm(x1, p["norm_ffn_g"], BF16)
    if prompt:
        h, tail = _ffn_up_prompt(xn2, p["w_gate"], p["w_up"], p["conv_w"], p["conv_b"], n_seq, tm=tm)
        conv_new = tail[:, SUBLANES - (CONV_W - 1):, :]
    else:
        h, t0, t1 = _ffn_up_sample(xn2, p["w_gate"], p["w_up"], p["conv_w"], p["conv_b"], conv_buf, n_seq)
        conv_new = jnp.stack([t0, t1], axis=1)
    down = _ffn_down(h, p["w_down"], tm)
    y = _rmsnorm(x1, p["final_norm_g"], F32, add=down)
    return y, ret_new, hg_new, conv_new


def kernel(x_prompt, x_sample, state_ret, state_hgrn, state_ffn_conv, norm_mix_g, w_in, ret_norm_g,
           hg_norm_g, hg_lb_logits, w_br_ret, w_br_hg, w_out, norm_ffn_g, w_gate, conv_w, conv_b,
           w_up, w_down, final_norm_g):
    assert norm_mix_g.shape[0] == 1, "single-layer trunk"
    bp, tp, d = x_prompt.shape
    bs, ts, _ = x_sample.shape
    p = dict(norm_mix_g=norm_mix_g[0], w_in=w_in[0], ret_norm_g=ret_norm_g[0], hg_norm_g=hg_norm_g[0],
             hg_lb_logits=hg_lb_logits, w_br_ret=w_br_ret[0], w_br_hg=w_br_hg[0], w_out=w_out[0],
             norm_ffn_g=norm_ffn_g[0], w_gate=w_gate[0], conv_w=conv_w[0], conv_b=conv_b[0],
             w_up=w_up[0], w_down=w_down[0], final_norm_g=final_norm_g)
    pos_p = jnp.arange(tp, dtype=F32)
    pos_s = PAST_LEN + jnp.arange(ts, dtype=F32)

    yp, rp, hp, cp = _trunk(x_prompt.reshape(bp * tp, d), bp, pos_p, None, None, None, p, 1, 1)
    ys, rs, hs, cs = _trunk(x_sample.reshape(bs * ts, d), bs, pos_s, state_ret[0], state_hgrn[0],
                            state_ffn_conv[0], p, 8, 8)
    return (yp.reshape(bp, tp, d), ys.reshape(bs, ts, d), rp[None], hp[None], cp[None],
            rs[None], hs[None], cs[None])
```

```python
import functools
import math

import jax
import jax.numpy as jnp
from jax import lax
from jax.experimental import pallas as pl
from jax.experimental.pallas import tpu as pltpu

F32 = jnp.float32
BF16 = jnp.bfloat16

D_MODEL = 4096
PAST_LEN = 16384
RET_HEADS = 8
RET_DK = 256
RET_DV = 256
RET_CHUNK = 128
ROPE_BASE = 10000.0
HG_HEADS = 16
HG_DK = 128
HG_DV = 128
HG_CHUNK = 16
D_FF = 11008
CONV_W = 3
EPS = 1e-6

RET_QK = RET_HEADS * RET_DK
RET_V = RET_HEADS * RET_DV
HG_K = HG_HEADS * HG_DK
HG_V = HG_HEADS * HG_DV
COL_QR = 0
COL_KR = COL_QR + RET_QK
COL_VR = COL_KR + RET_QK
COL_GR = COL_VR + RET_V
COL_FH = COL_GR + RET_V
COL_QH = COL_FH + HG_K
COL_IH = COL_QH + HG_K
COL_GH = COL_IH + HG_V
COL_GATE_RET = COL_GH + HG_V
COL_GATE_HG = COL_GATE_RET + D_MODEL
IN_COLS = COL_GATE_HG + D_MODEL

LANES = 128
SUBLANES = 8
V7X_MXU_DIM = 256
V7X_VMEM_BYTES = 64 * 1024 * 1024
MIB = 1024 * 1024

_NT = (((1,), (1,)), ((), ()))
_TN = (((0,), (0,)), ((), ()))


def _params(semantics, vmem_bytes):
    return pltpu.CompilerParams(dimension_semantics=semantics,
                                vmem_limit_bytes=min(int(vmem_bytes), V7X_VMEM_BYTES - 4 * MIB))


def _silu(x):
    return x * jax.nn.sigmoid(x)


def _rms_kernel(x_ref, g_ref, o_ref):
    x = x_ref[...]
    y = x * lax.rsqrt(jnp.mean(x * x, axis=-1, keepdims=True) + EPS)
    o_ref[...] = (y * g_ref[...]).astype(o_ref.dtype)


def _rms_add_kernel(x_ref, d_ref, g_ref, o_ref):
    x = x_ref[...] + d_ref[...]
    y = x * lax.rsqrt(jnp.mean(x * x, axis=-1, keepdims=True) + EPS)
    o_ref[...] = (y * g_ref[...]).astype(o_ref.dtype)


def _rmsnorm(x, g, out_dtype, add=None, tr=512):
    m, d = x.shape
    row = pl.BlockSpec((tr, d), lambda i: (i, 0))
    vec = pl.BlockSpec((1, d), lambda i: (0, 0))
    ins = [x] if add is None else [x, add]
    return pl.pallas_call(
        _rms_kernel if add is None else _rms_add_kernel,
        out_shape=jax.ShapeDtypeStruct((m, d), out_dtype),
        grid=(m // tr,),
        in_specs=[row] * len(ins) + [vec],
        out_specs=row,
        compiler_params=_params(("parallel",), 2 * (len(ins) + 2) * tr * d * 4),
        name="rmsnorm",
    )(*ins, g.reshape(1, d))


def _mm_kernel(x_ref, w_ref, o_ref):
    o_ref[...] = jnp.dot(x_ref[...], w_ref[...].astype(BF16), preferred_element_type=F32)


def _in_proj(xn, w, tm, tn=512):
    m, k = xn.shape
    n = w.shape[1]
    vmem = 2 * (tm * k * 2 + k * tn * 4 + tm * tn * 4) + k * tn * 2 + tm * tn * 4
    return pl.pallas_call(
        _mm_kernel,
        out_shape=jax.ShapeDtypeStruct((m, n), F32),
        grid=(m // tm, n // tn),
        in_specs=[pl.BlockSpec((tm, k), lambda i, j: (i, 0)),
                  pl.BlockSpec((k, tn), lambda i, j: (0, j))],
        out_specs=pl.BlockSpec((tm, tn), lambda i, j: (i, j)),
        compiler_params=_params(("parallel", "arbitrary"), vmem + 4 * MIB),
        name="in_proj",
    )(xn, w)


def _merge_kernel(or_ref, oh_ref, wr_ref, wh_ref, gr_ref, gh_ref, o_ref):
    pr = jnp.dot(or_ref[...], wr_ref[...].astype(BF16), preferred_element_type=F32)
    ph = jnp.dot(oh_ref[...], wh_ref[...].astype(BF16), preferred_element_type=F32)
    o_ref[...] = (jax.nn.sigmoid(gr_ref[...]) * pr + jax.nn.sigmoid(gh_ref[...]) * ph).astype(o_ref.dtype)


def _merge(o_r, o_h, w_br_ret, w_br_hg, proj, tm, tn=512):
    m = o_r.shape[0]
    kr, kh = o_r.shape[1], o_h.shape[1]
    n = D_MODEL
    vmem = 2 * (tm * (kr + kh) * 2 + (kr + kh) * tn * 4 + 2 * tm * tn * 4 + tm * tn * 2)
    vmem += (kr + kh) * tn * 2 + 3 * tm * tn * 4
    return pl.pallas_call(
        _merge_kernel,
        out_shape=jax.ShapeDtypeStruct((m, n), BF16),
        grid=(m // tm, n // tn),
        in_specs=[pl.BlockSpec((tm, kr), lambda i, j: (i, 0)),
                  pl.BlockSpec((tm, kh), lambda i, j: (i, 0)),
                  pl.BlockSpec((kr, tn), lambda i, j: (0, j)),
                  pl.BlockSpec((kh, tn), lambda i, j: (0, j)),
                  pl.BlockSpec((tm, tn), lambda i, j: (i, COL_GATE_RET // tn + j)),
                  pl.BlockSpec((tm, tn), lambda i, j: (i, COL_GATE_HG // tn + j))],
        out_specs=pl.BlockSpec((tm, tn), lambda i, j: (i, j)),
        compiler_params=_params(("parallel", "arbitrary"), vmem + 4 * MIB),
        name="merge",
    )(o_r, o_h, w_br_ret, w_br_hg, proj, proj)


def _out_proj_kernel(m_ref, w_ref, x_ref, o_ref):
    o_ref[...] = x_ref[...] + jnp.dot(m_ref[...], w_ref[...].astype(BF16), preferred_element_type=F32)


def _out_proj(merged, w, x, tm, tn=512):
    m, k = merged.shape
    n = w.shape[1]
    vmem = 2 * (tm * k * 2 + k * tn * 4 + 2 * tm * tn * 4) + k * tn * 2 + tm * tn * 4
    return pl.pallas_call(
        _out_proj_kernel,
        out_shape=jax.ShapeDtypeStruct((m, n), F32),
        grid=(m // tm, n // tn),
        in_specs=[pl.BlockSpec((tm, k), lambda i, j: (i, 0)),
                  pl.BlockSpec((k, tn), lambda i, j: (0, j)),
                  pl.BlockSpec((tm, tn), lambda i, j: (i, j))],
        out_specs=pl.BlockSpec((tm, tn), lambda i, j: (i, j)),
        compiler_params=_params(("parallel", "arbitrary"), vmem + 4 * MIB),
        name="out_proj",
    )(merged, w, x)


def _ffn_down_kernel(h_ref, w_ref, o_ref, *, k_last):
    k = pl.program_id(1)
    n_k = pl.num_programs(1)

    @pl.when(k == 0)
    def _():
        o_ref[...] = jnp.dot(h_ref[...], w_ref[...].astype(BF16), preferred_element_type=F32)

    @pl.when(jnp.logical_and(k > 0, k < n_k - 1))
    def _():
        o_ref[...] += jnp.dot(h_ref[...], w_ref[...].astype(BF16), preferred_element_type=F32)

    @pl.when(k == n_k - 1)
    def _():
        o_ref[...] += jnp.dot(h_ref[:, :k_last], w_ref[:k_last, :].astype(BF16), preferred_element_type=F32)


def _ffn_down(h, w, tm, tk=512):
    m, f = h.shape
    n = w.shape[1]
    n_k = pl.cdiv(f, tk)
    assert n_k >= 2
    vmem = 2 * (tm * tk * 2 + tk * n * 4 + tm * n * 4) + tk * n * 2 + tm * n * 4
    return pl.pallas_call(
        functools.partial(_ffn_down_kernel, k_last=f - (n_k - 1) * tk),
        out_shape=jax.ShapeDtypeStruct((m, n), F32),
        grid=(m // tm, n_k),
        in_specs=[pl.BlockSpec((tm, tk), lambda i, k: (i, k)),
                  pl.BlockSpec((tk, n), lambda i, k: (k, 0))],
        out_specs=pl.BlockSpec((tm, n), lambda i, k: (i, 0)),
        compiler_params=_params(("parallel", "arbitrary"), vmem + 4 * MIB),
        name="ffn_down",
    )(h, w)


def _conv_gate(full0, full1, full2, cw_ref, cb_ref, up):
    c = cb_ref[...] + ((full0 * cw_ref[0:1, :] + full1 * cw_ref[1:2, :]) + full2 * cw_ref[2:3, :])
    return _silu(c) * up


def _zero_after(x):
    r = jnp.max(x, axis=0, keepdims=True).astype(F32)
    bits = pltpu.bitcast(r, jnp.uint32)
    bits = lax.shift_right_logical(lax.shift_right_logical(bits, jnp.uint32(16)), jnp.uint32(16))
    return pltpu.bitcast(bits, F32).astype(BF16)


def _ffn_up_prompt_kernel(xn_ref, wg_ref, wu_ref, cw_ref, cb_ref, h_ref, tail_ref,
                          w_s, u_s, up_s, *, tm, tf, n_blocks, n_steps, blocks_per_seq):
    t = pl.program_id(0)
    i = lax.rem(t, n_blocks)
    n_kt = xn_ref.shape[1] // V7X_MXU_DIM
    rc = tm // n_kt

    def store_result(res):
        u_s[SUBLANES:SUBLANES + tm, :] = res[:, :tf]
        up_s[...] = res[:, tf:]

    def epilogue_rows(r0, nrows):
        h = _conv_gate(u_s[SUBLANES - 2 + r0:SUBLANES - 2 + r0 + nrows, :],
                       u_s[SUBLANES - 1 + r0:SUBLANES - 1 + r0 + nrows, :],
                       u_s[SUBLANES + r0:SUBLANES + r0 + nrows, :], cw_ref, cb_ref,
                       up_s[r0:r0 + nrows, :]).astype(h_ref.dtype)
        h_ref[r0:r0 + nrows, :] = h
        return h

    def finish_epilogue():
        last = u_s[tm:tm + SUBLANES, :]
        tail_ref[0] = last
        return last

    def cast_weights():
        w_s[:, :tf] = wg_ref[...].astype(BF16)
        w_s[:, tf:] = wu_ref[...].astype(BF16)

    @pl.when(t == 0)
    def _():
        cast_weights()
        u_s[0:SUBLANES, :] = jnp.zeros((SUBLANES, u_s.shape[1]), F32)
        store_result(jnp.dot(xn_ref[...], w_s[...], preferred_element_type=F32))

    def steady(first_of_tile):
        if first_of_tile:
            cast_weights()
        pieces = []
        for kt in range(n_kt):
            zero = _zero_after(epilogue_rows(kt * rc, rc))
            w_kt = w_s[kt * V7X_MXU_DIM:(kt + 1) * V7X_MXU_DIM, :]
            pieces.append(w_kt + jnp.concatenate([zero, zero], axis=1))
        last = finish_epilogue()
        res = jnp.dot(xn_ref[...], jnp.concatenate(pieces, axis=0), preferred_element_type=F32)
        u_s[0:SUBLANES, :] = jnp.where(lax.rem(i, blocks_per_seq) == 0, 0.0, last)
        store_result(res)

    @pl.when(jnp.logical_and(t > 0, jnp.logical_and(t < n_steps - 1, i == 0)))
    def _():
        steady(True)

    @pl.when(jnp.logical_and(t < n_steps - 1, i > 0))
    def _():
        steady(False)

    @pl.when(t == n_steps - 1)
    def _():
        epilogue_rows(0, tm)
        finish_epilogue()


def _ffn_up_prompt(xn, w_gate, w_up, conv_w, conv_b, n_seq, tm=1024, tf=256):
    m, d = xn.shape
    f = w_gate.shape[1]
    n_blocks = m // tm
    blocks_per_seq = n_blocks // n_seq
    n_steps = (f // tf) * n_blocks + 1
    assert d % V7X_MXU_DIM == 0 and tm % (d // V7X_MXU_DIM) == 0 and n_blocks >= 2
    vmem = 2 * (tm * d * 2 + 2 * d * tf * 4 + tm * tf * 2) + 4 * d * tf * 2 + (2 * tm + SUBLANES) * tf * 4
    vmem += 4 * tm * tf * 4
    cur = lambda t: jnp.minimum(t, n_steps - 2)
    prev = lambda t: jnp.maximum(t - 1, 0)
    up_tile = lambda t: jnp.minimum(cur(t) + 1, n_steps - 2) // n_blocks
    return pl.pallas_call(
        functools.partial(_ffn_up_prompt_kernel, tm=tm, tf=tf, n_blocks=n_blocks, n_steps=n_steps,
                          blocks_per_seq=blocks_per_seq),
        out_shape=(jax.ShapeDtypeStruct((m, f), BF16),
                   jax.ShapeDtypeStruct((n_seq, SUBLANES, f), F32)),
        grid=(n_steps,),
        in_specs=[pl.BlockSpec((tm, d), lambda t: (cur(t) % n_blocks, 0)),
                  pl.BlockSpec((d, tf), lambda t: (0, cur(t) // n_blocks)),
                  pl.BlockSpec((d, tf), lambda t: (0, up_tile(t))),
                  pl.BlockSpec((CONV_W, tf), lambda t: (0, prev(t) // n_blocks)),
                  pl.BlockSpec((1, tf), lambda t: (0, prev(t) // n_blocks))],
        out_specs=(pl.BlockSpec((tm, tf), lambda t: (prev(t) % n_blocks, prev(t) // n_blocks)),
                   pl.BlockSpec((1, SUBLANES, tf),
                                lambda t: ((prev(t) % n_blocks) // blocks_per_seq, 0, prev(t) // n_blocks))),
        scratch_shapes=[pltpu.VMEM((d, 2 * tf), BF16),
                        pltpu.VMEM((tm + SUBLANES, tf), F32), pltpu.VMEM((tm, tf), F32)],
        compiler_params=_params(("arbitrary",), vmem + 4 * MIB),
        name="ffn_up_prompt",
    )(xn, w_gate, w_up, conv_w, conv_b.reshape(1, f))


def _ffn_up_sample_kernel(xn_ref, wg_ref, wu_ref, cw_ref, cb_ref, buf_ref,
                          h_ref, tail_ref, u_s, up_s, h_s, *, n_seq, t_len):
    xn = xn_ref[...]
    u = jnp.dot(xn, wg_ref[...].astype(BF16), preferred_element_type=F32)
    up = jnp.dot(xn, wu_ref[...].astype(BF16), preferred_element_type=F32)
    for l in range(u_s.shape[0]):
        ls = slice(l * LANES, (l + 1) * LANES)
        u_s[l] = u[:, ls]
        up_s[l] = up[:, ls]
        full = [buf_ref[:, j, ls] for j in range(CONV_W - 1)]
        full += [u_s[l, pl.ds(t, n_seq, stride=t_len), :] for t in range(t_len)]
        for t in range(t_len):
            c = cb_ref[:, ls] + ((full[t] * cw_ref[0:1, ls] + full[t + 1] * cw_ref[1:2, ls])
                                 + full[t + 2] * cw_ref[2:3, ls])
            h_s[l, pl.ds(t, n_seq, stride=t_len), :] = _silu(c) * up_s[l, pl.ds(t, n_seq, stride=t_len), :]
        h_ref[:, ls] = h_s[l].astype(h_ref.dtype)
        for j in range(CONV_W - 1):
            tail_ref[:, j, ls] = full[t_len + j]


def _ffn_up_sample(xn, w_gate, w_up, conv_w, conv_b, conv_buf, n_seq, tf=256):
    m, d = xn.shape
    f = w_gate.shape[1]
    t_len = m // n_seq
    vmem = 2 * (m * d * 2 + 2 * d * tf * 4 + m * tf * 2 + 4 * n_seq * tf * 4) + 2 * d * tf * 2 + 5 * m * tf * 4
    return pl.pallas_call(
        functools.partial(_ffn_up_sample_kernel, n_seq=n_seq, t_len=t_len),
        out_shape=(jax.ShapeDtypeStruct((m, f), BF16),
                   jax.ShapeDtypeStruct((n_seq, CONV_W - 1, f), F32)),
        grid=(f // tf,),
        in_specs=[pl.BlockSpec((m, d), lambda j: (0, 0)),
                  pl.BlockSpec((d, tf), lambda j: (0, j)),
                  pl.BlockSpec((d, tf), lambda j: (0, j)),
                  pl.BlockSpec((CONV_W, tf), lambda j: (0, j)),
                  pl.BlockSpec((1, tf), lambda j: (0, j)),
                  pl.BlockSpec((n_seq, CONV_W - 1, tf), lambda j: (0, 0, j))],
        out_specs=(pl.BlockSpec((m, tf), lambda j: (0, j)),
                   pl.BlockSpec((n_seq, CONV_W - 1, tf), lambda j: (0, 0, j))),
        scratch_shapes=[pltpu.VMEM((tf // LANES, m, LANES), F32)] * 3,
        compiler_params=_params(("parallel",), vmem + 4 * MIB),
        name="ffn_up_sample",
    )(xn, w_gate, w_up, conv_w, conv_b.reshape(1, f), conv_buf)


def _retention_kernel(lg_ref, q_ref, k_ref, v_ref, g_ref, cos_ref, sin_ref, gn_ref, *rest,
                      chunk, n_seg, heads, carry):
    if carry:
        o_ref, sout_ref = rest
    else:
        s0_ref, o_ref, sout_ref = rest
    rows = chunk * n_seg
    shift = int(math.log2(chunk))
    half = RET_DK // 2
    cos = cos_ref[...]
    sin = sin_ref[...]

    def rot(x):
        x1 = x[:, :half]
        x2 = x[:, half:]
        return jnp.concatenate([x1 * cos - x2 * sin, x1 * sin + x2 * cos], axis=-1)

    ri = lax.broadcasted_iota(jnp.int32, (rows, rows), 0)
    ci = lax.broadcasted_iota(jnp.int32, (rows, rows), 1)
    ok = ri >= ci
    if n_seg > 1:
        ok = jnp.logical_and(ok, lax.shift_right_logical(ri, shift) == lax.shift_right_logical(ci, shift))
    diff = jnp.where(ok, (ri - ci).astype(F32), 0.0)
    row_id = lax.broadcasted_iota(jnp.int32, (rows, RET_DK), 0)
    pos = (row_id & (chunk - 1)).astype(F32)
    seg = lax.shift_right_logical(row_id, shift)

    if carry:
        @pl.when(pl.program_id(2) == 0)
        def _():
            sout_ref[...] = jnp.zeros_like(sout_ref)

    for h in range(heads):
        cs = slice(h * RET_DK, (h + 1) * RET_DK)
        log_g = lg_ref[pl.program_id(1) * heads + h]
        dmat = jnp.where(ok, jnp.exp(diff * log_g), 0.0)
        q_dec = jnp.exp((pos + 1.0) * log_g)
        k_dec = jnp.exp((chunk - 1.0 - pos) * log_g)
        s_dec = jnp.exp(jnp.full((1, RET_DV), float(chunk), F32) * log_g)

        qb = rot(q_ref[:, cs]).astype(BF16)
        kr = rot(k_ref[:, cs]) * (RET_DK ** -0.5)
        vb = v_ref[:, cs].astype(BF16)
        scores = lax.dot_general(qb, kr.astype(BF16), _NT, preferred_element_type=F32) * dmat
        o = jnp.dot(scores.astype(BF16), vb, preferred_element_type=F32)
        kd = kr * k_dec
        if carry:
            s = sout_ref[0, h]
            o = o + jnp.dot(qb, s.astype(BF16), preferred_element_type=F32) * q_dec
            sout_ref[0, h] = s * s_dec + lax.dot_general(kd.astype(BF16), vb, _TN, preferred_element_type=F32)
        else:
            for b in range(n_seg):
                mine = seg == b
                s = s0_ref[b, h]
                o_b = jnp.dot(qb, s.astype(BF16), preferred_element_type=F32) * q_dec
                o = o + jnp.where(mine, o_b, 0.0)
                kd_b = jnp.where(mine, kd, 0.0).astype(BF16)
                sout_ref[b, h] = s * s_dec + lax.dot_general(kd_b, vb, _TN, preferred_element_type=F32)

        y = o * lax.rsqrt(jnp.mean(o * o, axis=-1, keepdims=True) + EPS) * gn_ref[:, cs]
        o_ref[:, cs] = (y * _silu(g_ref[:, cs])).astype(o_ref.dtype)


def _retention(proj, log_g, cos, sin, ret_norm_g, state, n_seq, chunk, n_seg, heads):
    m = proj.shape[0]
    rows = chunk * n_seg
    carry = state is None
    width = heads * RET_DK
    cq, ck, cv, cg = (c // width for c in (COL_QR, COL_KR, COL_VR, COL_GR))
    n_hg = RET_HEADS // heads
    if carry:
        n_chunks = m // n_seq // rows
        grid = (n_seq, n_hg, n_chunks)
        row_blk = lambda b, hg, c, lg: b * n_chunks + c
        tab_blk = lambda b, hg, c, lg: (c, 0)
        st_blk = lambda b, hg, c, lg: (b, hg, 0, 0)
        sem = ("parallel", "parallel", "arbitrary")
    else:
        grid = (m // rows, n_hg)
        row_blk = lambda i, hg, lg: i
        tab_blk = lambda i, hg, lg: (0, 0)
        st_blk = lambda i, hg, lg: (i, hg, 0, 0)
        sem = ("parallel", "parallel")
    st_shape = (n_seg, heads, RET_DK, RET_DV)

    def col(c0):
        return pl.BlockSpec((rows, width), lambda *a: (row_blk(*a), c0 + a[1]))

    in_specs = [col(cq), col(ck), col(cv), col(cg),
                pl.BlockSpec((rows, RET_DK // 2), tab_blk),
                pl.BlockSpec((rows, RET_DK // 2), tab_blk),
                pl.BlockSpec((1, width), lambda *a: (0, a[1]))]
    args = [proj, proj, proj, proj, cos, sin, ret_norm_g.reshape(1, RET_V)]
    if not carry:
        in_specs.append(pl.BlockSpec(st_shape, st_blk))
        args.append(state)
    st_bytes = n_seg * heads * RET_DK * RET_DV * 4
    vmem = 2 * (6 * rows * width * 4 + 2 * st_bytes) + 16 * rows * RET_DK * 4 + 4 * RET_DK * RET_DV * 4
    return pl.pallas_call(
        functools.partial(_retention_kernel, chunk=chunk, n_seg=n_seg, heads=heads, carry=carry),
        out_shape=(jax.ShapeDtypeStruct((m, RET_V), BF16),
                   jax.ShapeDtypeStruct((n_seq, RET_HEADS, RET_DK, RET_DV), F32)),
        grid_spec=pltpu.PrefetchScalarGridSpec(
            num_scalar_prefetch=1, grid=grid, in_specs=in_specs,
            out_specs=(pl.BlockSpec((rows, width), lambda *a: (row_blk(*a), a[1])),
                       pl.BlockSpec(st_shape, st_blk))),
        compiler_params=_params(sem, vmem + 8 * MIB),
        name="retention_prompt" if carry else "retention_sample",
    )(log_g, *args)


def _hg_gates(f_raw, i_raw, lb):
    f = lb + (1.0 - lb) * jax.nn.sigmoid(f_raw)
    return jnp.log(f), 1.0 - f, _silu(i_raw)


def _hg_lower_bound(lb_ref):
    logits = lb_ref[...]
    e = jnp.exp(logits - jnp.max(logits, axis=0, keepdims=True))
    return e[0:1, :] / jnp.sum(e, axis=0, keepdims=True)


def _hg_cumsum(log_f, pos, chunk):
    b = log_f
    sh = 1
    while sh < chunk:
        b = b + jnp.where(pos >= sh, pltpu.roll(b, sh, 0), 0.0)
        sh *= 2
    return b


def _hg_intra(q, k, v, b, pos, chunk):
    o = jnp.sum(q * k, axis=-1, keepdims=True) * v
    for d in range(1, chunk):
        valid = pos >= d
        decay = jnp.exp(jnp.where(valid, b - pltpu.roll(b, d, 0), -jnp.inf))
        score = jnp.sum(q * decay * pltpu.roll(k, d, 0), axis=-1, keepdims=True)
        o = o + score * pltpu.roll(v, d, 0)
    return o


def _hg_finish(o, g_raw, gn):
    y = o * lax.rsqrt(jnp.mean(o * o, axis=-1, keepdims=True) + EPS) * gn
    return y * _silu(g_raw)


def _hg_intra_scores(q, k, b2, chunk, lane, row):
    n_blk = chunk // SUBLANES
    qs = [q[i * SUBLANES:(i + 1) * SUBLANES] for i in range(n_blk)]
    bs = [b2[i * SUBLANES:(i + 1) * SUBLANES] for i in range(n_blk)]
    c2 = b2 - jnp.log2(jnp.maximum(k, 0.0))
    p = [jnp.zeros((SUBLANES, HG_DK), F32) for _ in range(n_blk)]
    for s in range(chunk):
        c_s = jnp.broadcast_to(c2[s:s + 1, :], (SUBLANES, HG_DK))
        for i in range(s // SUBLANES, n_blk):
            col = jnp.sum(qs[i] * jnp.exp2(bs[i] - c_s), axis=-1, keepdims=True)
            sel = lane == s
            if i == s // SUBLANES:
                sel = jnp.logical_and(sel, row >= (s - i * SUBLANES))
            p[i] = jnp.where(sel, col, p[i])
    return jnp.concatenate(p, axis=0)


def _hgrn_prompt_kernel(f_ref, q_ref, i_ref, g_ref, lb_ref, gn_ref, o_ref, sout_ref,
                        st_ref, p_s, qd_s, kd_s, v_s, e_s, *, heads, rows, chunk):
    tb = pl.program_id(2)

    @pl.when(tb == 0)
    def _():
        st_ref[...] = jnp.zeros_like(st_ref)

    lb = _hg_lower_bound(lb_ref)
    gn = gn_ref[...]
    pos = lax.broadcasted_iota(jnp.int32, (chunk, HG_DK), 0)
    lane = lax.broadcasted_iota(jnp.int32, (SUBLANES, HG_DK), 1)
    row = lax.broadcasted_iota(jnp.int32, (SUBLANES, HG_DK), 0)

    def front(c):
        rs = pl.ds(pl.multiple_of(c * chunk, chunk), chunk)
        for h in range(heads):
            cs = slice(h * HG_DK, (h + 1) * HG_DK)
            q = q_ref[rs, cs]
            f = lb[:, cs] + (1.0 - lb[:, cs]) * jax.nn.sigmoid(f_ref[rs, cs])
            k = 1.0 - f
            b2 = _hg_cumsum(jnp.log2(f), pos, chunk)
            p_s[h] = _hg_intra_scores(q, k, b2, chunk, lane, row).astype(BF16)
            b_last = b2[chunk - 1:chunk, :]
            qd_s[h] = (q * jnp.exp2(b2)).astype(BF16)
            kd_s[h] = (k * jnp.exp2(b_last - b2)).astype(BF16)
            v_s[h] = _silu(i_ref[rs, cs]).astype(BF16)
            e_s[h] = jnp.broadcast_to(jnp.exp2(b_last), (SUBLANES, HG_DK))

    def back(c):
        rs = pl.ds(pl.multiple_of(c * chunk, chunk), chunk)
        for h in range(heads):
            cs = slice(h * HG_DK, (h + 1) * HG_DK)
            vb = v_s[h]
            st = st_ref[h]
            o = jnp.dot(p_s[h][:, :chunk], vb, preferred_element_type=F32)
            o = o + lax.dot_general(qd_s[h], st.astype(BF16), _NT, preferred_element_type=F32)
            st_ref[h] = st * e_s[h][0:1, :] + lax.dot_general(vb, kd_s[h], _TN, preferred_element_type=F32)
            o_ref[rs, cs] = _hg_finish(o, g_ref[rs, cs], gn[:, cs]).astype(o_ref.dtype)

    def body(c, carry):
        back(c - 1)
        front(c)
        return carry

    n_chunks = rows // chunk
    front(0)
    lax.fori_loop(1, n_chunks, body, 0)
    back(n_chunks - 1)

    @pl.when(tb == pl.num_programs(2) - 1)
    def _():
        for h in range(heads):
            sout_ref[0, h] = st_ref[h].T


def _hgrn_prompt(proj, lb_logits, hg_norm_g, n_seq, chunk, heads=16, rows=512):
    m = proj.shape[0]
    n_tb = m // n_seq // rows
    width = heads * HG_DK
    cf, cq, ci, cg = (c // width for c in (COL_FH, COL_QH, COL_IH, COL_GH))

    def col(c0):
        return pl.BlockSpec((rows, width), lambda b, hg, t: (b * n_tb + t, c0 + hg))

    vmem = 2 * (4 * rows * width * 4 + rows * width * 2 + heads * HG_DK * HG_DV * 4) + heads * HG_DK * HG_DV * 4
    return pl.pallas_call(
        functools.partial(_hgrn_prompt_kernel, heads=heads, rows=rows, chunk=chunk),
        out_shape=(jax.ShapeDtypeStruct((m, HG_V), BF16),
                   jax.ShapeDtypeStruct((n_seq, HG_HEADS, HG_DK, HG_DV), F32)),
        grid=(n_seq, HG_HEADS // heads, n_tb),
        in_specs=[col(cf), col(cq), col(ci), col(cg),
                  pl.BlockSpec((lb_logits.shape[0], width), lambda b, hg, t: (0, hg)),
                  pl.BlockSpec((1, width), lambda b, hg, t: (0, hg))],
        out_specs=(pl.BlockSpec((rows, width), lambda b, hg, t: (b * n_tb + t, hg)),
                   pl.BlockSpec((1, heads, HG_DK, HG_DV), lambda b, hg, t: (b, hg, 0, 0))),
        scratch_shapes=[pltpu.VMEM((heads, HG_DV, HG_DK), F32)]
                       + [pltpu.VMEM((heads, chunk, HG_DK), BF16)] * 4
                       + [pltpu.VMEM((heads, SUBLANES, HG_DK), F32)],
        compiler_params=_params(("parallel", "parallel", "arbitrary"), vmem + 8 * MIB),
        name="hgrn_prompt",
    )(proj, proj, proj, proj, lb_logits, hg_norm_g.reshape(1, HG_V))


def _hgrn_sample_kernel(f_ref, q_ref, i_ref, g_ref, lb_ref, gn_ref, s0_ref, o_ref, sout_ref,
                        *, heads, n_seg, chunk):
    rows = n_seg * chunk
    shift = int(math.log2(chunk))
    lb = _hg_lower_bound(lb_ref)
    gn = gn_ref[...]
    row_id = lax.broadcasted_iota(jnp.int32, (rows, HG_DK), 0)
    pos = row_id & (chunk - 1)
    seg = lax.shift_right_logical(row_id, shift)
    eye = (lax.broadcasted_iota(jnp.int32, (HG_DK, HG_DK), 0)
           == lax.broadcasted_iota(jnp.int32, (HG_DK, HG_DK), 1))
    for h in range(heads):
        cs = slice(h * HG_DK, (h + 1) * HG_DK)
        q = q_ref[:, cs]
        log_f, k, v = _hg_gates(f_ref[:, cs], i_ref[:, cs], lb[:, cs])
        b = _hg_cumsum(log_f, pos, chunk)
        o = _hg_intra(q, k, v, b, pos, chunk)
        b_last = b
        for j in range(1, chunk):
            b_last = jnp.where(pos == chunk - 1 - j, pltpu.roll(b, rows - j, 0), b_last)
        q_dec = (q * jnp.exp(b)).astype(BF16)
        k_dec = k * jnp.exp(b_last - b)
        vb = v.astype(BF16)
        for s in range(n_seg):
            mine = seg == s
            st = s0_ref[s, h]
            o = o + jnp.where(mine, jnp.dot(q_dec, st.astype(BF16), preferred_element_type=F32), 0.0)
            e_row = jnp.exp(b[(s + 1) * chunk - 1:(s + 1) * chunk, :])
            e_col = jnp.sum(jnp.where(eye, e_row, 0.0), axis=-1, keepdims=True)
            k_s = jnp.where(mine, k_dec, 0.0).astype(BF16)
            sout_ref[s, h] = e_col * st + lax.dot_general(k_s, vb, _TN, preferred_element_type=F32)
        o_ref[:, cs] = _hg_finish(o, g_ref[:, cs], gn[:, cs]).astype(o_ref.dtype)


def _hgrn_sample(proj, lb_logits, hg_norm_g, state, chunk, heads=8, n_seg=8):
    m = proj.shape[0]
    n_seq = m // chunk
    rows = n_seg * chunk
    width = heads * HG_DK
    cf, cq, ci, cg = (c // width for c in (COL_FH, COL_QH, COL_IH, COL_GH))

    def col(c0):
        return pl.BlockSpec((rows, width), lambda i, hg: (i, c0 + hg))

    st_spec = pl.BlockSpec((n_seg, heads, HG_DK, HG_DV), lambda i, hg: (i, hg, 0, 0))
    st_bytes = n_seg * heads * HG_DK * HG_DV * 4
    vmem = 2 * (5 * rows * width * 4 + 2 * st_bytes)
    return pl.pallas_call(
        functools.partial(_hgrn_sample_kernel, heads=heads, n_seg=n_seg, chunk=chunk),
        out_shape=(jax.ShapeDtypeStruct((m, HG_V), BF16),
                   jax.ShapeDtypeStruct((n_seq, HG_HEADS, HG_DK, HG_DV), F32)),
        grid=(m // rows, HG_HEADS // heads),
        in_specs=[col(cf), col(cq), col(ci), col(cg),
                  pl.BlockSpec((lb_logits.shape[0], width), lambda i, hg: (0, hg)),
                  pl.BlockSpec((1, width), lambda i, hg: (0, hg)),
                  st_spec],
        out_specs=(pl.BlockSpec((rows, width), lambda i, hg: (i, hg)), st_spec),
        compiler_params=_params(("parallel", "parallel"), vmem + 8 * MIB),
        name="hgrn_sample",
    )(proj, proj, proj, proj, lb_logits, hg_norm_g.reshape(1, HG_V), state)


def _rope_tables(pos):
    half = RET_DK // 2
    inv = ROPE_BASE ** (-jnp.arange(half, dtype=F32) / half)
    ang = pos[:, None] * inv[None, :]
    return jnp.cos(ang), jnp.sin(ang)


def _trunk(x, n_seq, pos, ret_state, hg_state, conv_buf, p, ret_seg, hg_seg):
    m = x.shape[0]
    t_len = m // n_seq
    prompt = ret_state is None
    tm = min(m, 1024)
    ret_chunk = math.gcd(t_len, RET_CHUNK)
    hg_chunk = math.gcd(t_len, HG_CHUNK)
    log_g = jnp.log1p(-jnp.exp2(-5.0 - jnp.arange(RET_HEADS, dtype=F32)))
    cos, sin = _rope_tables(pos)
    if not prompt:
        cos = jnp.tile(cos, (ret_seg, 1))
        sin = jnp.tile(sin, (ret_seg, 1))

    xn = _rmsnorm(x, p["norm_mix_g"], BF16)
    proj = _in_proj(xn, p["w_in"], tm, tn=768 if prompt else 1024)
    o_r, ret_new = _retention(proj, log_g, cos, sin, p["ret_norm_g"], ret_state, n_seq, ret_chunk,
                              1 if prompt else ret_seg, RET_HEADS if prompt else 2)
    if prompt:
        o_h, hg_new = _hgrn_prompt(proj, p["hg_lb_logits"], p["hg_norm_g"], n_seq, hg_chunk)
    else:
        o_h, hg_new = _hgrn_sample(proj, p["hg_lb_logits"], p["hg_norm_g"], hg_state, hg_chunk, n_seg=hg_seg)
    merged = _merge(o_r, o_h, p["w_br_ret"], p["w_br_hg"], proj, tm)
    x1 = _out_proj(merged, p["w_out"], x, tm)

    xn2 = _rmsnorm(x1, p["norm_ffn_g"], BF16)
    if prompt:
        h, tail = _ffn_up_prompt(xn2, p["w_gate"], p["w_up"], p["conv_w"], p["conv_b"], n_seq, tm=tm)
        conv_new = tail[:, SUBLANES - (CONV_W - 1):, :]
    else:
        h, conv_new = _ffn_up_sample(xn2, p["w_gate"], p["w_up"], p["conv_w"], p["conv_b"], conv_buf, n_seq)
    down = _ffn_down(h, p["w_down"], tm)
    y = _rmsnorm(x1, p["final_norm_g"], F32, add=down)
    return y, ret_new, hg_new, conv_new


def kernel(x_prompt, x_sample, state_ret, state_hgrn, state_ffn_conv, norm_mix_g, w_in, ret_norm_g,
           hg_norm_g, hg_lb_logits, w_br_ret, w_br_hg, w_out, norm_ffn_g, w_gate, conv_w, conv_b,
           w_up, w_down, final_norm_g):
    assert norm_mix_g.shape[0] == 1, "single-layer trunk"
    bp, tp, d = x_prompt.shape
    bs, ts, _ = x_sample.shape
    p = dict(norm_mix_g=norm_mix_g[0], w_in=w_in[0], ret_norm_g=ret_norm_g[0], hg_norm_g=hg_norm_g[0],
             hg_lb_logits=hg_lb_logits, w_br_ret=w_br_ret[0], w_br_hg=w_br_hg[0], w_out=w_out[0],
             norm_ffn_g=norm_ffn_g[0], w_gate=w_gate[0], conv_w=conv_w[0], conv_b=conv_b[0],
             w_up=w_up[0], w_down=w_down[0], final_norm_g=final_norm_g)
    pos_p = jnp.arange(tp, dtype=F32)
    pos_s = PAST_LEN + jnp.arange(ts, dtype=F32)

    yp, rp, hp, cp = _trunk(x_prompt.reshape(bp * tp, d), bp, pos_p, None, None, None, p, 1, 1)
    ys, rs, hs, cs = _trunk(x_sample.reshape(bs * ts, d), bs, pos_s, state_ret[0], state_hgrn[0],
                            state_ffn_conv[0], p, 8, 8)
    return (yp.reshape(bp, tp, d), ys.reshape(bs, ts, d), rp[None], hp[None], cp[None],
            rs[None], hs[None], cs[None])
```

```python
import functools
import math

import jax
import jax.numpy as jnp
from jax import lax
from jax.experimental import pallas as pl
from jax.experimental.pallas import tpu as pltpu

F32 = jnp.float32
BF16 = jnp.bfloat16

D_MODEL = 4096
PAST_LEN = 16384
RET_HEADS = 8
RET_DK = 256
RET_DV = 256
RET_CHUNK = 128
ROPE_BASE = 10000.0
HG_HEADS = 16
HG_DK = 128
HG_DV = 128
HG_CHUNK = 16
D_FF = 11008
CONV_W = 3
EPS = 1e-6

RET_QK = RET_HEADS * RET_DK
RET_V = RET_HEADS * RET_DV
HG_K = HG_HEADS * HG_DK
HG_V = HG_HEADS * HG_DV
COL_QR = 0
COL_KR = COL_QR + RET_QK
COL_VR = COL_KR + RET_QK
COL_GR = COL_VR + RET_V
COL_FH = COL_GR + RET_V
COL_QH = COL_FH + HG_K
COL_IH = COL_QH + HG_K
COL_GH = COL_IH + HG_V
COL_GATE_RET = COL_GH + HG_V
COL_GATE_HG = COL_GATE_RET + D_MODEL
IN_COLS = COL_GATE_HG + D_MODEL

LANES = 128
SUBLANES = 8
V7X_MXU_DIM = 256
V7X_VMEM_BYTES = 64 * 1024 * 1024
MIB = 1024 * 1024

_NT = (((1,), (1,)), ((), ()))
_TN = (((0,), (0,)), ((), ()))


def _params(semantics, vmem_bytes):
    return pltpu.CompilerParams(dimension_semantics=semantics,
                                vmem_limit_bytes=min(int(vmem_bytes), V7X_VMEM_BYTES - 4 * MIB))


def _silu(x):
    return x * jax.nn.sigmoid(x)


def _rms_kernel(x_ref, g_ref, o_ref):
    x = x_ref[...]
    y = x * lax.rsqrt(jnp.mean(x * x, axis=-1, keepdims=True) + EPS)
    o_ref[...] = (y * g_ref[...]).astype(o_ref.dtype)


def _rms_add_kernel(x_ref, d_ref, g_ref, o_ref):
    x = x_ref[...] + d_ref[...]
    y = x * lax.rsqrt(jnp.mean(x * x, axis=-1, keepdims=True) + EPS)
    o_ref[...] = (y * g_ref[...]).astype(o_ref.dtype)


def _rmsnorm(x, g, out_dtype, add=None, tr=512):
    m, d = x.shape
    tr = min(tr, m // 2)
    row = pl.BlockSpec((tr, d), lambda i: (i, 0))
    vec = pl.BlockSpec((1, d), lambda i: (0, 0))
    ins = [x] if add is None else [x, add]
    return pl.pallas_call(
        _rms_kernel if add is None else _rms_add_kernel,
        out_shape=jax.ShapeDtypeStruct((m, d), out_dtype),
        grid=(m // tr,),
        in_specs=[row] * len(ins) + [vec],
        out_specs=row,
        compiler_params=_params(("parallel",), 2 * (len(ins) + 2) * tr * d * 4),
        name="rmsnorm",
    )(*ins, g.reshape(1, d))


def _mm_kernel(x_ref, w_ref, o_ref):
    o_ref[...] = jnp.dot(x_ref[...], w_ref[...].astype(BF16), preferred_element_type=F32)


def _in_proj(xn, w, tm, tn=512):
    m, k = xn.shape
    n = w.shape[1]
    vmem = 2 * (tm * k * 2 + k * tn * 4 + tm * tn * 4) + k * tn * 2 + tm * tn * 4
    return pl.pallas_call(
        _mm_kernel,
        out_shape=jax.ShapeDtypeStruct((m, n), F32),
        grid=(m // tm, n // tn),
        in_specs=[pl.BlockSpec((tm, k), lambda i, j: (i, 0)),
                  pl.BlockSpec((k, tn), lambda i, j: (0, j))],
        out_specs=pl.BlockSpec((tm, tn), lambda i, j: (i, j)),
        compiler_params=_params(("parallel", "arbitrary"), vmem + 4 * MIB),
        name="in_proj",
    )(xn, w)


def _merge_kernel(or_ref, oh_ref, wr_ref, wh_ref, gr_ref, gh_ref, o_ref):
    pr = jnp.dot(or_ref[...], wr_ref[...].astype(BF16), preferred_element_type=F32)
    ph = jnp.dot(oh_ref[...], wh_ref[...].astype(BF16), preferred_element_type=F32)
    o_ref[...] = (jax.nn.sigmoid(gr_ref[...]) * pr + jax.nn.sigmoid(gh_ref[...]) * ph).astype(o_ref.dtype)


def _merge(o_r, o_h, w_br_ret, w_br_hg, proj, tm, tn=512):
    m = o_r.shape[0]
    kr, kh = o_r.shape[1], o_h.shape[1]
    n = D_MODEL
    vmem = 2 * (tm * (kr + kh) * 2 + (kr + kh) * tn * 4 + 2 * tm * tn * 4 + tm * tn * 2)
    vmem += (kr + kh) * tn * 2 + 3 * tm * tn * 4
    return pl.pallas_call(
        _merge_kernel,
        out_shape=jax.ShapeDtypeStruct((m, n), BF16),
        grid=(m // tm, n // tn),
        in_specs=[pl.BlockSpec((tm, kr), lambda i, j: (i, 0)),
                  pl.BlockSpec((tm, kh), lambda i, j: (i, 0)),
                  pl.BlockSpec((kr, tn), lambda i, j: (0, j)),
                  pl.BlockSpec((kh, tn), lambda i, j: (0, j)),
                  pl.BlockSpec((tm, tn), lambda i, j: (i, COL_GATE_RET // tn + j)),
                  pl.BlockSpec((tm, tn), lambda i, j: (i, COL_GATE_HG // tn + j))],
        out_specs=pl.BlockSpec((tm, tn), lambda i, j: (i, j)),
        compiler_params=_params(("parallel", "arbitrary"), vmem + 4 * MIB),
        name="merge",
    )(o_r, o_h, w_br_ret, w_br_hg, proj, proj)


def _out_proj_kernel(m_ref, w_ref, x_ref, o_ref):
    o_ref[...] = x_ref[...] + jnp.dot(m_ref[...], w_ref[...].astype(BF16), preferred_element_type=F32)


def _out_proj(merged, w, x, tm, tn=512):
    m, k = merged.shape
    n = w.shape[1]
    vmem = 2 * (tm * k * 2 + k * tn * 4 + 2 * tm * tn * 4) + k * tn * 2 + tm * tn * 4
    return pl.pallas_call(
        _out_proj_kernel,
        out_shape=jax.ShapeDtypeStruct((m, n), F32),
        grid=(m // tm, n // tn),
        in_specs=[pl.BlockSpec((tm, k), lambda i, j: (i, 0)),
                  pl.BlockSpec((k, tn), lambda i, j: (0, j)),
                  pl.BlockSpec((tm, tn), lambda i, j: (i, j))],
        out_specs=pl.BlockSpec((tm, tn), lambda i, j: (i, j)),
        compiler_params=_params(("parallel", "arbitrary"), vmem + 4 * MIB),
        name="out_proj",
    )(merged, w, x)


def _ffn_down_kernel(h_ref, w_ref, o_ref, *, k_last):
    k = pl.program_id(1)
    n_k = pl.num_programs(1)

    @pl.when(k == 0)
    def _():
        o_ref[...] = jnp.dot(h_ref[...], w_ref[...].astype(BF16), preferred_element_type=F32)

    @pl.when(jnp.logical_and(k > 0, k < n_k - 1))
    def _():
        o_ref[...] += jnp.dot(h_ref[...], w_ref[...].astype(BF16), preferred_element_type=F32)

    @pl.when(k == n_k - 1)
    def _():
        o_ref[...] += jnp.dot(h_ref[:, :k_last], w_ref[:k_last, :].astype(BF16), preferred_element_type=F32)


def _ffn_down(h, w, tm, tk=512):
    m, f = h.shape
    n = w.shape[1]
    n_k = pl.cdiv(f, tk)
    assert n_k >= 2
    vmem = 2 * (tm * tk * 2 + tk * n * 4 + tm * n * 4) + tk * n * 2 + tm * n * 4
    return pl.pallas_call(
        functools.partial(_ffn_down_kernel, k_last=f - (n_k - 1) * tk),
        out_shape=jax.ShapeDtypeStruct((m, n), F32),
        grid=(m // tm, n_k),
        in_specs=[pl.BlockSpec((tm, tk), lambda i, k: (i, k)),
                  pl.BlockSpec((tk, n), lambda i, k: (k, 0))],
        out_specs=pl.BlockSpec((tm, n), lambda i, k: (i, 0)),
        compiler_params=_params(("parallel", "arbitrary"), vmem + 4 * MIB),
        name="ffn_down",
    )(h, w)


def _conv_gate(full0, full1, full2, cw_ref, cb_ref, up):
    c = cb_ref[...] + ((full0 * cw_ref[0:1, :] + full1 * cw_ref[1:2, :]) + full2 * cw_ref[2:3, :])
    return _silu(c) * up


def _zero_after(x):
    r = jnp.max(x, axis=0, keepdims=True).astype(F32)
    bits = pltpu.bitcast(r, jnp.uint32)
    bits = lax.shift_right_logical(lax.shift_right_logical(bits, jnp.uint32(16)), jnp.uint32(16))
    return pltpu.bitcast(bits, F32).astype(BF16)


def _ffn_up_prompt_kernel(xn_ref, wg_ref, wu_ref, cw_ref, cb_ref, h_ref, tail_ref,
                          w_s, u_s, up_s, *, tm, tf, n_blocks, n_steps, blocks_per_seq):
    t = pl.program_id(0)
    i = lax.rem(t, n_blocks)
    n_kt = xn_ref.shape[1] // V7X_MXU_DIM
    rc = tm // n_kt

    def store_result(res):
        u_s[SUBLANES:SUBLANES + tm, :] = res[:, :tf]
        up_s[...] = res[:, tf:]

    def epilogue_rows(r0, nrows):
        h = _conv_gate(u_s[SUBLANES - 2 + r0:SUBLANES - 2 + r0 + nrows, :],
                       u_s[SUBLANES - 1 + r0:SUBLANES - 1 + r0 + nrows, :],
                       u_s[SUBLANES + r0:SUBLANES + r0 + nrows, :], cw_ref, cb_ref,
                       up_s[r0:r0 + nrows, :]).astype(h_ref.dtype)
        h_ref[r0:r0 + nrows, :] = h
        return h

    def finish_epilogue():
        last = u_s[tm:tm + SUBLANES, :]
        tail_ref[0] = last
        return last

    def cast_weights():
        w_s[:, :tf] = wg_ref[...].astype(BF16)
        w_s[:, tf:] = wu_ref[...].astype(BF16)

    @pl.when(t == 0)
    def _():
        cast_weights()
        u_s[0:SUBLANES, :] = jnp.zeros((SUBLANES, u_s.shape[1]), F32)
        store_result(jnp.dot(xn_ref[...], w_s[...], preferred_element_type=F32))

    def steady(first_of_tile):
        if first_of_tile:
            cast_weights()
        pieces = []
        for kt in range(n_kt):
            zero = _zero_after(epilogue_rows(kt * rc, rc))
            w_kt = w_s[kt * V7X_MXU_DIM:(kt + 1) * V7X_MXU_DIM, :]
            pieces.append(w_kt + jnp.concatenate([zero, zero], axis=1))
        last = finish_epilogue()
        res = jnp.dot(xn_ref[...], jnp.concatenate(pieces, axis=0), preferred_element_type=F32)
        u_s[0:SUBLANES, :] = jnp.where(lax.rem(i, blocks_per_seq) == 0, 0.0, last)
        store_result(res)

    @pl.when(jnp.logical_and(t > 0, jnp.logical_and(t < n_steps - 1, i == 0)))
    def _():
        steady(True)

    @pl.when(jnp.logical_and(t < n_steps - 1, i > 0))
    def _():
        steady(False)

    @pl.when(t == n_steps - 1)
    def _():
        epilogue_rows(0, tm)
        finish_epilogue()


def _ffn_up_prompt(xn, w_gate, w_up, conv_w, conv_b, n_seq, tm=1024, tf=256):
    m, d = xn.shape
    f = w_gate.shape[1]
    n_blocks = m // tm
    blocks_per_seq = n_blocks // n_seq
    n_steps = (f // tf) * n_blocks + 1
    assert d % V7X_MXU_DIM == 0 and tm % (d // V7X_MXU_DIM) == 0 and n_blocks >= 2
    vmem = 2 * (tm * d * 2 + 2 * d * tf * 4 + tm * tf * 2) + 4 * d * tf * 2 + (2 * tm + SUBLANES) * tf * 4
    vmem += 4 * tm * tf * 4
    cur = lambda t: jnp.minimum(t, n_steps - 2)
    prev = lambda t: jnp.maximum(t - 1, 0)
    up_tile = lambda t: jnp.minimum(cur(t) + 1, n_steps - 2) // n_blocks
    return pl.pallas_call(
        functools.partial(_ffn_up_prompt_kernel, tm=tm, tf=tf, n_blocks=n_blocks, n_steps=n_steps,
                          blocks_per_seq=blocks_per_seq),
        out_shape=(jax.ShapeDtypeStruct((m, f), BF16),
                   jax.ShapeDtypeStruct((n_seq, SUBLANES, f), F32)),
        grid=(n_steps,),
        in_specs=[pl.BlockSpec((tm, d), lambda t: (cur(t) % n_blocks, 0)),
                  pl.BlockSpec((d, tf), lambda t: (0, cur(t) // n_blocks)),
                  pl.BlockSpec((d, tf), lambda t: (0, up_tile(t))),
                  pl.BlockSpec((CONV_W, tf), lambda t: (0, prev(t) // n_blocks)),
                  pl.BlockSpec((1, tf), lambda t: (0, prev(t) // n_blocks))],
        out_specs=(pl.BlockSpec((tm, tf), lambda t: (prev(t) % n_blocks, prev(t) // n_blocks)),
                   pl.BlockSpec((1, SUBLANES, tf),
                                lambda t: ((prev(t) % n_blocks) // blocks_per_seq, 0, prev(t) // n_blocks))),
        scratch_shapes=[pltpu.VMEM((d, 2 * tf), BF16),
                        pltpu.VMEM((tm + SUBLANES, tf), F32), pltpu.VMEM((tm, tf), F32)],
        compiler_params=_params(("arbitrary",), vmem + 4 * MIB),
        name="ffn_up_prompt",
    )(xn, w_gate, w_up, conv_w, conv_b.reshape(1, f))


def _ffn_up_sample_kernel(xn_ref, wg_ref, wu_ref, cw_ref, cb_ref, buf_ref,
                          h_ref, tail_ref, u_s, up_s, h_s, *, n_seq, t_len):
    xn = xn_ref[...]
    u = jnp.dot(xn, wg_ref[...].astype(BF16), preferred_element_type=F32)
    up = jnp.dot(xn, wu_ref[...].astype(BF16), preferred_element_type=F32)
    for l in range(u_s.shape[0]):
        ls = slice(l * LANES, (l + 1) * LANES)
        u_s[l] = u[:, ls]
        up_s[l] = up[:, ls]
        full = [buf_ref[:, j, ls] for j in range(CONV_W - 1)]
        full += [u_s[l, pl.ds(t, n_seq, stride=t_len), :] for t in range(t_len)]
        for t in range(t_len):
            c = cb_ref[:, ls] + ((full[t] * cw_ref[0:1, ls] + full[t + 1] * cw_ref[1:2, ls])
                                 + full[t + 2] * cw_ref[2:3, ls])
            h_s[l, pl.ds(t, n_seq, stride=t_len), :] = _silu(c) * up_s[l, pl.ds(t, n_seq, stride=t_len), :]
        h_ref[:, ls] = h_s[l].astype(h_ref.dtype)
        for j in range(CONV_W - 1):
            tail_ref[:, j, ls] = full[t_len + j]


def _ffn_up_sample(xn, w_gate, w_up, conv_w, conv_b, conv_buf, n_seq, tf=256):
    m, d = xn.shape
    f = w_gate.shape[1]
    t_len = m // n_seq
    vmem = 2 * (m * d * 2 + 2 * d * tf * 4 + m * tf * 2 + 4 * n_seq * tf * 4) + 2 * d * tf * 2 + 5 * m * tf * 4
    return pl.pallas_call(
        functools.partial(_ffn_up_sample_kernel, n_seq=n_seq, t_len=t_len),
        out_shape=(jax.ShapeDtypeStruct((m, f), BF16),
                   jax.ShapeDtypeStruct((n_seq, CONV_W - 1, f), F32)),
        grid=(f // tf,),
        in_specs=[pl.BlockSpec((m, d), lambda j: (0, 0)),
                  pl.BlockSpec((d, tf), lambda j: (0, j)),
                  pl.BlockSpec((d, tf), lambda j: (0, j)),
                  pl.BlockSpec((CONV_W, tf), lambda j: (0, j)),
                  pl.BlockSpec((1, tf), lambda j: (0, j)),
                  pl.BlockSpec((n_seq, CONV_W - 1, tf), lambda j: (0, 0, j))],
        out_specs=(pl.BlockSpec((m, tf), lambda j: (0, j)),
                   pl.BlockSpec((n_seq, CONV_W - 1, tf), lambda j: (0, 0, j))),
        scratch_shapes=[pltpu.VMEM((tf // LANES, m, LANES), F32)] * 3,
        compiler_params=_params(("parallel",), vmem + 4 * MIB),
        name="ffn_up_sample",
    )(xn, w_gate, w_up, conv_w, conv_b.reshape(1, f), conv_buf)


def _retention_kernel(lg_ref, q_ref, k_ref, v_ref, g_ref, cos_ref, sin_ref, gn_ref, *rest,
                      chunk, n_seg, heads, carry):
    if carry:
        o_ref, sout_ref = rest
    else:
        s0_ref, o_ref, sout_ref = rest
    rows = chunk * n_seg
    shift = int(math.log2(chunk))
    half = RET_DK // 2
    cos = cos_ref[...]
    sin = sin_ref[...]

    def rot(x):
        x1 = x[:, :half]
        x2 = x[:, half:]
        return jnp.concatenate([x1 * cos - x2 * sin, x1 * sin + x2 * cos], axis=-1)

    ri = lax.broadcasted_iota(jnp.int32, (rows, rows), 0)
    ci = lax.broadcasted_iota(jnp.int32, (rows, rows), 1)
    ok = ri >= ci
    if n_seg > 1:
        ok = jnp.logical_and(ok, lax.shift_right_logical(ri, shift) == lax.shift_right_logical(ci, shift))
    diff = jnp.where(ok, (ri - ci).astype(F32), 0.0)
    row_id = lax.broadcasted_iota(jnp.int32, (rows, RET_DK), 0)
    pos = (row_id & (chunk - 1)).astype(F32)
    seg = lax.shift_right_logical(row_id, shift)

    if carry:
        @pl.when(pl.program_id(2) == 0)
        def _():
            sout_ref[...] = jnp.zeros_like(sout_ref)

    for h in range(heads):
        cs = slice(h * RET_DK, (h + 1) * RET_DK)
        log_g = lg_ref[pl.program_id(1) * heads + h]
        dmat = jnp.where(ok, jnp.exp(diff * log_g), 0.0)
        q_dec = jnp.exp((pos + 1.0) * log_g)
        k_dec = jnp.exp((chunk - 1.0 - pos) * log_g)
        s_dec = jnp.exp(jnp.full((1, RET_DV), float(chunk), F32) * log_g)

        qb = rot(q_ref[:, cs]).astype(BF16)
        kr = rot(k_ref[:, cs]) * (RET_DK ** -0.5)
        vb = v_ref[:, cs].astype(BF16)
        scores = lax.dot_general(qb, kr.astype(BF16), _NT, preferred_element_type=F32) * dmat
        o = jnp.dot(scores.astype(BF16), vb, preferred_element_type=F32)
        kd = kr * k_dec
        if carry:
            s = sout_ref[0, h]
            o = o + jnp.dot(qb, s.astype(BF16), preferred_element_type=F32) * q_dec
            sout_ref[0, h] = s * s_dec + lax.dot_general(kd.astype(BF16), vb, _TN, preferred_element_type=F32)
        else:
            for b in range(n_seg):
                mine = seg == b
                s = s0_ref[b, h]
                o_b = jnp.dot(qb, s.astype(BF16), preferred_element_type=F32) * q_dec
                o = o + jnp.where(mine, o_b, 0.0)
                kd_b = jnp.where(mine, kd, 0.0).astype(BF16)
                sout_ref[b, h] = s * s_dec + lax.dot_general(kd_b, vb, _TN, preferred_element_type=F32)

        y = o * lax.rsqrt(jnp.mean(o * o, axis=-1, keepdims=True) + EPS) * gn_ref[:, cs]
        o_ref[:, cs] = (y * _silu(g_ref[:, cs])).astype(o_ref.dtype)


def _retention(proj, log_g, cos, sin, ret_norm_g, state, n_seq, chunk, n_seg, heads):
    m = proj.shape[0]
    rows = chunk * n_seg
    carry = state is None
    width = heads * RET_DK
    cq, ck, cv, cg = (c // width for c in (COL_QR, COL_KR, COL_VR, COL_GR))
    n_hg = RET_HEADS // heads
    if carry:
        n_chunks = m // n_seq // rows
        grid = (n_seq, n_hg, n_chunks)
        row_blk = lambda b, hg, c, lg: b * n_chunks + c
        tab_blk = lambda b, hg, c, lg: (c, 0)
        st_blk = lambda b, hg, c, lg: (b, hg, 0, 0)
        sem = ("parallel", "parallel", "arbitrary")
    else:
        grid = (m // rows, n_hg)
        row_blk = lambda i, hg, lg: i
        tab_blk = lambda i, hg, lg: (0, 0)
        st_blk = lambda i, hg, lg: (i, hg, 0, 0)
        sem = ("parallel", "parallel")
    st_shape = (n_seg, heads, RET_DK, RET_DV)

    def col(c0):
        return pl.BlockSpec((rows, width), lambda *a: (row_blk(*a), c0 + a[1]))

    in_specs = [col(cq), col(ck), col(cv), col(cg),
                pl.BlockSpec((rows, RET_DK // 2), tab_blk),
                pl.BlockSpec((rows, RET_DK // 2), tab_blk),
                pl.BlockSpec((1, width), lambda *a: (0, a[1]))]
    args = [proj, proj, proj, proj, cos, sin, ret_norm_g.reshape(1, RET_V)]
    if not carry:
        in_specs.append(pl.BlockSpec(st_shape, st_blk))
        args.append(state)
    st_bytes = n_seg * heads * RET_DK * RET_DV * 4
    vmem = 2 * (6 * rows * width * 4 + 2 * st_bytes) + 16 * rows * RET_DK * 4 + 4 * RET_DK * RET_DV * 4
    return pl.pallas_call(
        functools.partial(_retention_kernel, chunk=chunk, n_seg=n_seg, heads=heads, carry=carry),
        out_shape=(jax.ShapeDtypeStruct((m, RET_V), BF16),
                   jax.ShapeDtypeStruct((n_seq, RET_HEADS, RET_DK, RET_DV), F32)),
        grid_spec=pltpu.PrefetchScalarGridSpec(
            num_scalar_prefetch=1, grid=grid, in_specs=in_specs,
            out_specs=(pl.BlockSpec((rows, width), lambda *a: (row_blk(*a), a[1])),
                       pl.BlockSpec(st_shape, st_blk))),
        compiler_params=_params(sem, vmem + 8 * MIB),
        name="retention_prompt" if carry else "retention_sample",
    )(log_g, *args)


def _hg_gates(f_raw, i_raw, lb):
    f = lb + (1.0 - lb) * jax.nn.sigmoid(f_raw)
    return jnp.log(f), 1.0 - f, _silu(i_raw)


def _hg_lower_bound(lb_ref):
    logits = lb_ref[...]
    e = jnp.exp(logits - jnp.max(logits, axis=0, keepdims=True))
    return e[0:1, :] / jnp.sum(e, axis=0, keepdims=True)


def _hg_cumsum(log_f, pos, chunk):
    b = log_f
    sh = 1
    while sh < chunk:
        b = b + jnp.where(pos >= sh, pltpu.roll(b, sh, 0), 0.0)
        sh *= 2
    return b


def _hg_intra(q, k, v, b, pos, chunk):
    o = jnp.sum(q * k, axis=-1, keepdims=True) * v
    for d in range(1, chunk):
        valid = pos >= d
        decay = jnp.exp(jnp.where(valid, b - pltpu.roll(b, d, 0), -jnp.inf))
        score = jnp.sum(q * decay * pltpu.roll(k, d, 0), axis=-1, keepdims=True)
        o = o + score * pltpu.roll(v, d, 0)
    return o


def _hg_finish(o, g_raw, gn):
    y = o * lax.rsqrt(jnp.mean(o * o, axis=-1, keepdims=True) + EPS) * gn
    return y * _silu(g_raw)


def _hg_intra_scores(q, k, b2, chunk, lane, row):
    n_blk = chunk // SUBLANES
    qs = [q[i * SUBLANES:(i + 1) * SUBLANES] for i in range(n_blk)]
    bs = [b2[i * SUBLANES:(i + 1) * SUBLANES] for i in range(n_blk)]
    c2 = b2 - jnp.log2(jnp.maximum(k, 0.0))
    p = [jnp.zeros((SUBLANES, HG_DK), F32) for _ in range(n_blk)]
    for s in range(chunk):
        c_s = jnp.broadcast_to(c2[s:s + 1, :], (SUBLANES, HG_DK))
        for i in range(s // SUBLANES, n_blk):
            col = jnp.sum(qs[i] * jnp.exp2(bs[i] - c_s), axis=-1, keepdims=True)
            sel = lane == s
            if i == s // SUBLANES:
                sel = jnp.logical_and(sel, row >= (s - i * SUBLANES))
            p[i] = jnp.where(sel, col, p[i])
    return jnp.concatenate(p, axis=0)


def _hgrn_prompt_kernel(f_ref, q_ref, i_ref, g_ref, lb_ref, gn_ref, o_ref, sout_ref,
                        st_ref, p_s, qd_s, kd_s, v_s, e_s, *, heads, rows, chunk):
    tb = pl.program_id(2)

    @pl.when(tb == 0)
    def _():
        st_ref[...] = jnp.zeros_like(st_ref)

    lb = _hg_lower_bound(lb_ref)
    gn = gn_ref[...]
    pos = lax.broadcasted_iota(jnp.int32, (chunk, HG_DK), 0)
    lane = lax.broadcasted_iota(jnp.int32, (SUBLANES, HG_DK), 1)
    row = lax.broadcasted_iota(jnp.int32, (SUBLANES, HG_DK), 0)

    def front(c):
        rs = pl.ds(pl.multiple_of(c * chunk, chunk), chunk)
        for h in range(heads):
            cs = slice(h * HG_DK, (h + 1) * HG_DK)
            q = q_ref[rs, cs]
            f = lb[:, cs] + (1.0 - lb[:, cs]) * jax.nn.sigmoid(f_ref[rs, cs])
            k = 1.0 - f
            b2 = _hg_cumsum(jnp.log2(f), pos, chunk)
            p_s[h] = _hg_intra_scores(q, k, b2, chunk, lane, row).astype(BF16)
            b_last = b2[chunk - 1:chunk, :]
            qd_s[h] = (q * jnp.exp2(b2)).astype(BF16)
            kd_s[h] = (k * jnp.exp2(b_last - b2)).astype(BF16)
            v_s[h] = _silu(i_ref[rs, cs]).astype(BF16)
            e_s[h] = jnp.broadcast_to(jnp.exp2(b_last), (SUBLANES, HG_DK))

    def back(c):
        rs = pl.ds(pl.multiple_of(c * chunk, chunk), chunk)
        for h in range(heads):
            cs = slice(h * HG_DK, (h + 1) * HG_DK)
            vb = v_s[h]
            st = st_ref[h]
            o = jnp.dot(p_s[h][:, :chunk], vb, preferred_element_type=F32)
            o = o + lax.dot_general(qd_s[h], st.astype(BF16), _NT, preferred_element_type=F32)
            st_ref[h] = st * e_s[h][0:1, :] + lax.dot_general(vb, kd_s[h], _TN, preferred_element_type=F32)
            o_ref[rs, cs] = _hg_finish(o, g_ref[rs, cs], gn[:, cs]).astype(o_ref.dtype)

    def body(c, carry):
        back(c - 1)
        front(c)
        return carry

    n_chunks = rows // chunk
    front(0)
    lax.fori_loop(1, n_chunks, body, 0)
    back(n_chunks - 1)

    @pl.when(tb == pl.num_programs(2) - 1)
    def _():
        for h in range(heads):
            sout_ref[0, h] = st_ref[h].T


def _hgrn_prompt(proj, lb_logits, hg_norm_g, n_seq, chunk, heads=16, rows=512):
    m = proj.shape[0]
    n_tb = m // n_seq // rows
    width = heads * HG_DK
    cf, cq, ci, cg = (c // width for c in (COL_FH, COL_QH, COL_IH, COL_GH))

    def col(c0):
        return pl.BlockSpec((rows, width), lambda b, hg, t: (b * n_tb + t, c0 + hg))

    vmem = 2 * (4 * rows * width * 4 + rows * width * 2 + heads * HG_DK * HG_DV * 4) + heads * HG_DK * HG_DV * 4
    return pl.pallas_call(
        functools.partial(_hgrn_prompt_kernel, heads=heads, rows=rows, chunk=chunk),
        out_shape=(jax.ShapeDtypeStruct((m, HG_V), BF16),
                   jax.ShapeDtypeStruct((n_seq, HG_HEADS, HG_DK, HG_DV), F32)),
        grid=(n_seq, HG_HEADS // heads, n_tb),
        in_specs=[col(cf), col(cq), col(ci), col(cg),
                  pl.BlockSpec((lb_logits.shape[0], width), lambda b, hg, t: (0, hg)),
                  pl.BlockSpec((1, width), lambda b, hg, t: (0, hg))],
        out_specs=(pl.BlockSpec((rows, width), lambda b, hg, t: (b * n_tb + t, hg)),
                   pl.BlockSpec((1, heads, HG_DK, HG_DV), lambda b, hg, t: (b, hg, 0, 0))),
        scratch_shapes=[pltpu.VMEM((heads, HG_DV, HG_DK), F32)]
                       + [pltpu.VMEM((heads, chunk, HG_DK), BF16)] * 4
                       + [pltpu.VMEM((heads, SUBLANES, HG_DK), F32)],
        compiler_params=_params(("parallel", "parallel", "arbitrary"), vmem + 8 * MIB),
        name="hgrn_prompt",
    )(proj, proj, proj, proj, lb_logits, hg_norm_g.reshape(1, HG_V))


def _hgrn_sample_kernel(f_ref, q_ref, i_ref, g_ref, lb_ref, gn_ref, s0_ref, o_ref, sout_ref,
                        *, heads, n_seg, chunk):
    rows = n_seg * chunk
    shift = int(math.log2(chunk))
    lb = _hg_lower_bound(lb_ref)
    gn = gn_ref[...]
    row_id = lax.broadcasted_iota(jnp.int32, (rows, HG_DK), 0)
    pos = row_id & (chunk - 1)
    seg = lax.shift_right_logical(row_id, shift)
    eye = (lax.broadcasted_iota(jnp.int32, (HG_DK, HG_DK), 0)
           == lax.broadcasted_iota(jnp.int32, (HG_DK, HG_DK), 1))
    for h in range(heads):
        cs = slice(h * HG_DK, (h + 1) * HG_DK)
        q = q_ref[:, cs]
        log_f, k, v = _hg_gates(f_ref[:, cs], i_ref[:, cs], lb[:, cs])
        b = _hg_cumsum(log_f, pos, chunk)
        o = _hg_intra(q, k, v, b, pos, chunk)
        b_last = b
        for j in range(1, chunk):
            b_last = jnp.where(pos == chunk - 1 - j, pltpu.roll(b, rows - j, 0), b_last)
        q_dec = (q * jnp.exp(b)).astype(BF16)
        k_dec = k * jnp.exp(b_last - b)
        vb = v.astype(BF16)
        for s in range(n_seg):
            mine = seg == s
            st = s0_ref[s, h]
            o = o + jnp.where(mine, jnp.dot(q_dec, st.astype(BF16), preferred_element_type=F32), 0.0)
            e_row = jnp.exp(b[(s + 1) * chunk - 1:(s + 1) * chunk, :])
            e_col = jnp.sum(jnp.where(eye, e_row, 0.0), axis=-1, keepdims=True)
            k_s = jnp.where(mine, k_dec, 0.0).astype(BF16)
            sout_ref[s, h] = e_col * st + lax.dot_general(k_s, vb, _TN, preferred_element_type=F32)
        o_ref[:, cs] = _hg_finish(o, g_ref[:, cs], gn[:, cs]).astype(o_ref.dtype)


def _hgrn_sample(proj, lb_logits, hg_norm_g, state, chunk, heads=16, n_seg=8):
    m = proj.shape[0]
    n_seq = m // chunk
    rows = n_seg * chunk
    width = heads * HG_DK
    cf, cq, ci, cg = (c // width for c in (COL_FH, COL_QH, COL_IH, COL_GH))

    def col(c0):
        return pl.BlockSpec((rows, width), lambda i, hg: (i, c0 + hg))

    st_spec = pl.BlockSpec((n_seg, heads, HG_DK, HG_DV), lambda i, hg: (i, hg, 0, 0))
    st_bytes = n_seg * heads * HG_DK * HG_DV * 4
    vmem = 2 * (5 * rows * width * 4 + 2 * st_bytes)
    return pl.pallas_call(
        functools.partial(_hgrn_sample_kernel, heads=heads, n_seg=n_seg, chunk=chunk),
        out_shape=(jax.ShapeDtypeStruct((m, HG_V), BF16),
                   jax.ShapeDtypeStruct((n_seq, HG_HEADS, HG_DK, HG_DV), F32)),
        grid=(m // rows, HG_HEADS // heads),
        in_specs=[col(cf), col(cq), col(ci), col(cg),
                  pl.BlockSpec((lb_logits.shape[0], width), lambda i, hg: (0, hg)),
                  pl.BlockSpec((1, width), lambda i, hg: (0, hg)),
                  st_spec],
        out_specs=(pl.BlockSpec((rows, width), lambda i, hg: (i, hg)), st_spec),
        compiler_params=_params(("parallel", "parallel"), vmem + 8 * MIB),
        name="hgrn_sample",
    )(proj, proj, proj, proj, lb_logits, hg_norm_g.reshape(1, HG_V), state)


def _rope_tables(pos):
    half = RET_DK // 2
    inv = ROPE_BASE ** (-jnp.arange(half, dtype=F32) / half)
    ang = pos[:, None] * inv[None, :]
    return jnp.cos(ang), jnp.sin(ang)


def _trunk(x, n_seq, pos, ret_state, hg_state, conv_buf, p, ret_seg, hg_seg):
    m = x.shape[0]
    t_len = m // n_seq
    prompt = ret_state is None
    tm = min(m, 1024)
    ret_chunk = math.gcd(t_len, RET_CHUNK)
    hg_chunk = math.gcd(t_len, HG_CHUNK)
    log_g = jnp.log1p(-jnp.exp2(-5.0 - jnp.arange(RET_HEADS, dtype=F32)))
    cos, sin = _rope_tables(pos)
    if not prompt:
        cos = jnp.tile(cos, (ret_seg, 1))
        sin = jnp.tile(sin, (ret_seg, 1))

    xn = _rmsnorm(x, p["norm_mix_g"], BF16)
    proj = _in_proj(xn, p["w_in"], tm, tn=768 if prompt else 1024)
    o_r, ret_new = _retention(proj, log_g, cos, sin, p["ret_norm_g"], ret_state, n_seq, ret_chunk,
                              1 if prompt else ret_seg, RET_HEADS if prompt else 4)
    if prompt:
        o_h, hg_new = _hgrn_prompt(proj, p["hg_lb_logits"], p["hg_norm_g"], n_seq, hg_chunk)
    else:
        o_h, hg_new = _hgrn_sample(proj, p["hg_lb_logits"], p["hg_norm_g"], hg_state, hg_chunk, n_seg=hg_seg)
    merged = _merge(o_r, o_h, p["w_br_ret"], p["w_br_hg"], proj, tm)
    x1 = _out_proj(merged, p["w_out"], x, tm)

    xn2 = _rmsnorm(x1, p["norm_ffn_g"], BF16)
    if prompt:
        h, tail = _ffn_up_prompt(xn2, p["w_gate"], p["w_up"], p["conv_w"], p["conv_b"], n_seq, tm=tm)
        conv_new = tail[:, SUBLANES - (CONV_W - 1):, :]
    else:
        h, conv_new = _ffn_up_sample(xn2, p["w_gate"], p["w_up"], p["conv_w"], p["conv_b"], conv_buf, n_seq)
    down = _ffn_down(h, p["w_down"], tm)
    y = _rmsnorm(x1, p["final_norm_g"], F32, add=down)
    return y, ret_new, hg_new, conv_new


def kernel(x_prompt, x_sample, state_ret, state_hgrn, state_ffn_conv, norm_mix_g, w_in, ret_norm_g,
           hg_norm_g, hg_lb_logits, w_br_ret, w_br_hg, w_out, norm_ffn_g, w_gate, conv_w, conv_b,
           w_up, w_down, final_norm_g):
    assert norm_mix_g.shape[0] == 1, "single-layer trunk"
    bp, tp, d = x_prompt.shape
    bs, ts, _ = x_sample.shape
    p = dict(norm_mix_g=norm_mix_g[0], w_in=w_in[0], ret_norm_g=ret_norm_g[0], hg_norm_g=hg_norm_g[0],
             hg_lb_logits=hg_lb_logits, w_br_ret=w_br_ret[0], w_br_hg=w_br_hg[0], w_out=w_out[0],
             norm_ffn_g=norm_ffn_g[0], w_gate=w_gate[0], conv_w=conv_w[0], conv_b=conv_b[0],
             w_up=w_up[0], w_down=w_down[0], final_norm_g=final_norm_g)
    pos_p = jnp.arange(tp, dtype=F32)
    pos_s = PAST_LEN + jnp.arange(ts, dtype=F32)

    yp, rp, hp, cp = _trunk(x_prompt.reshape(bp * tp, d), bp, pos_p, None, None, None, p, 1, 1)
    ys, rs, hs, cs = _trunk(x_sample.reshape(bs * ts, d), bs, pos_s, state_ret[0], state_hgrn[0],
                            state_ffn_conv[0], p, 8, 8)
    return (yp.reshape(bp, tp, d), ys.reshape(bs, ts, d), rp[None], hp[None], cp[None],
            rs[None], hs[None], cs[None])
```

```python
import functools
import math

import jax
import jax.numpy as jnp
from jax import lax
from jax.experimental import pallas as pl
from jax.experimental.pallas import tpu as pltpu

F32 = jnp.float32
BF16 = jnp.bfloat16

D_MODEL = 4096
PAST_LEN = 16384
RET_HEADS = 8
RET_DK = 256
RET_DV = 256
RET_CHUNK = 128
ROPE_BASE = 10000.0
HG_HEADS = 16
HG_DK = 128
HG_DV = 128
HG_CHUNK = 16
D_FF = 11008
CONV_W = 3
EPS = 1e-6

RET_QK = RET_HEADS * RET_DK
RET_V = RET_HEADS * RET_DV
HG_K = HG_HEADS * HG_DK
HG_V = HG_HEADS * HG_DV
COL_QR = 0
COL_KR = COL_QR + RET_QK
COL_VR = COL_KR + RET_QK
COL_GR = COL_VR + RET_V
COL_FH = COL_GR + RET_V
COL_QH = COL_FH + HG_K
COL_IH = COL_QH + HG_K
COL_GH = COL_IH + HG_V
COL_GATE_RET = COL_GH + HG_V
COL_GATE_HG = COL_GATE_RET + D_MODEL
IN_COLS = COL_GATE_HG + D_MODEL

LANES = 128
SUBLANES = 8
V7X_MXU_DIM = 256
V7X_VMEM_BYTES = 64 * 1024 * 1024
MIB = 1024 * 1024

_NT = (((1,), (1,)), ((), ()))
_TN = (((0,), (0,)), ((), ()))


def _params(semantics, vmem_bytes):
    return pltpu.CompilerParams(dimension_semantics=semantics,
                                vmem_limit_bytes=min(int(vmem_bytes), V7X_VMEM_BYTES - 4 * MIB))


def _silu(x):
    return x * jax.nn.sigmoid(x)


def _rms_kernel(x_ref, g_ref, o_ref):
    x = x_ref[...]
    y = x * lax.rsqrt(jnp.mean(x * x, axis=-1, keepdims=True) + EPS)
    o_ref[...] = (y * g_ref[...]).astype(o_ref.dtype)


def _rms_add_kernel(x_ref, d_ref, g_ref, o_ref):
    x = x_ref[...] + d_ref[...]
    y = x * lax.rsqrt(jnp.mean(x * x, axis=-1, keepdims=True) + EPS)
    o_ref[...] = (y * g_ref[...]).astype(o_ref.dtype)


def _rmsnorm(x, g, out_dtype, add=None, tr=512):
    m, d = x.shape
    tr = min(tr, m // 2)
    row = pl.BlockSpec((tr, d), lambda i: (i, 0))
    vec = pl.BlockSpec((1, d), lambda i: (0, 0))
    ins = [x] if add is None else [x, add]
    return pl.pallas_call(
        _rms_kernel if add is None else _rms_add_kernel,
        out_shape=jax.ShapeDtypeStruct((m, d), out_dtype),
        grid=(m // tr,),
        in_specs=[row] * len(ins) + [vec],
        out_specs=row,
        compiler_params=_params(("parallel",), 2 * (len(ins) + 2) * tr * d * 4),
        name="rmsnorm",
    )(*ins, g.reshape(1, d))


def _mm_kernel(x_ref, w_ref, o_ref):
    o_ref[...] = jnp.dot(x_ref[...], w_ref[...].astype(BF16), preferred_element_type=F32)


def _in_proj(xn, w, tm, tn=512):
    m, k = xn.shape
    n = w.shape[1]
    vmem = 2 * (tm * k * 2 + k * tn * 4 + tm * tn * 4) + k * tn * 2 + tm * tn * 4
    return pl.pallas_call(
        _mm_kernel,
        out_shape=jax.ShapeDtypeStruct((m, n), F32),
        grid=(m // tm, n // tn),
        in_specs=[pl.BlockSpec((tm, k), lambda i, j: (i, 0)),
                  pl.BlockSpec((k, tn), lambda i, j: (0, j))],
        out_specs=pl.BlockSpec((tm, tn), lambda i, j: (i, j)),
        compiler_params=_params(("parallel", "arbitrary"), vmem + 4 * MIB),
        name="in_proj",
    )(xn, w)


def _merge_kernel(or_ref, oh_ref, wr_ref, wh_ref, gr_ref, gh_ref, o_ref, pr_s, ph_s, *, n_steps):
    t = pl.program_id(0)
    tm = o_ref.shape[0]
    n_r = or_ref.shape[1] // V7X_MXU_DIM
    n_h = oh_ref.shape[1] // V7X_MXU_DIM
    rc = tm // (n_r + n_h)

    def gate(rs):
        return (jax.nn.sigmoid(gr_ref[rs, :]) * pr_s[rs, :]
                + jax.nn.sigmoid(gh_ref[rs, :]) * ph_s[rs, :]).astype(o_ref.dtype)

    @pl.when(t == 0)
    def _():
        pr_s[...] = jnp.dot(or_ref[...], wr_ref[...].astype(BF16), preferred_element_type=F32)
        ph_s[...] = jnp.dot(oh_ref[...], wh_ref[...].astype(BF16), preferred_element_type=F32)

    @pl.when(jnp.logical_and(t > 0, t < n_steps - 1))
    def _():
        pieces = []
        for idx in range(n_r + n_h):
            rs = slice(idx * rc, (idx + 1) * rc)
            e = gate(rs)
            o_ref[rs, :] = e
            w_ref, kt = (wr_ref, idx) if idx < n_r else (wh_ref, idx - n_r)
            w_kt = w_ref[kt * V7X_MXU_DIM:(kt + 1) * V7X_MXU_DIM, :].astype(BF16)
            pieces.append(w_kt + _zero_after(e))
        pr = jnp.dot(or_ref[...], jnp.concatenate(pieces[:n_r], axis=0), preferred_element_type=F32)
        ph = jnp.dot(oh_ref[...], jnp.concatenate(pieces[n_r:], axis=0), preferred_element_type=F32)
        pr_s[...] = pr
        ph_s[...] = ph

    @pl.when(t == n_steps - 1)
    def _():
        o_ref[...] = gate(slice(0, tm))


def _merge(o_r, o_h, w_br_ret, w_br_hg, proj, tm, tn=512):
    m = o_r.shape[0]
    kr, kh = o_r.shape[1], o_h.shape[1]
    n = D_MODEL
    n_j = n // tn
    n_steps = (m // tm) * n_j + 1
    assert kr % V7X_MXU_DIM == 0 and kh % V7X_MXU_DIM == 0 and tm % ((kr + kh) // V7X_MXU_DIM) == 0
    cur = lambda t: jnp.minimum(t, n_steps - 2)
    prev = lambda t: jnp.maximum(t - 1, 0)
    vmem = 2 * (tm * (kr + kh) * 2 + (kr + kh) * tn * 4 + 2 * tm * tn * 4 + tm * tn * 2)
    vmem += (kr + kh) * tn * 2 + 5 * tm * tn * 4
    return pl.pallas_call(
        functools.partial(_merge_kernel, n_steps=n_steps),
        out_shape=jax.ShapeDtypeStruct((m, n), BF16),
        grid=(n_steps,),
        in_specs=[pl.BlockSpec((tm, kr), lambda t: (cur(t) // n_j, 0)),
                  pl.BlockSpec((tm, kh), lambda t: (cur(t) // n_j, 0)),
                  pl.BlockSpec((kr, tn), lambda t: (0, cur(t) % n_j)),
                  pl.BlockSpec((kh, tn), lambda t: (0, cur(t) % n_j)),
                  pl.BlockSpec((tm, tn), lambda t: (prev(t) // n_j, COL_GATE_RET // tn + prev(t) % n_j)),
                  pl.BlockSpec((tm, tn), lambda t: (prev(t) // n_j, COL_GATE_HG // tn + prev(t) % n_j))],
        out_specs=pl.BlockSpec((tm, tn), lambda t: (prev(t) // n_j, prev(t) % n_j)),
        scratch_shapes=[pltpu.VMEM((tm, tn), F32), pltpu.VMEM((tm, tn), F32)],
        compiler_params=_params(("arbitrary",), vmem + 4 * MIB),
        name="merge",
    )(o_r, o_h, w_br_ret, w_br_hg, proj, proj)


def _out_proj_kernel(m_ref, w_ref, x_ref, o_ref):
    o_ref[...] = x_ref[...] + jnp.dot(m_ref[...], w_ref[...].astype(BF16), preferred_element_type=F32)


def _out_proj(merged, w, x, tm, tn=512):
    m, k = merged.shape
    n = w.shape[1]
    vmem = 2 * (tm * k * 2 + k * tn * 4 + 2 * tm * tn * 4) + k * tn * 2 + tm * tn * 4
    return pl.pallas_call(
        _out_proj_kernel,
        out_shape=jax.ShapeDtypeStruct((m, n), F32),
        grid=(m // tm, n // tn),
        in_specs=[pl.BlockSpec((tm, k), lambda i, j: (i, 0)),
                  pl.BlockSpec((k, tn), lambda i, j: (0, j)),
                  pl.BlockSpec((tm, tn), lambda i, j: (i, j))],
        out_specs=pl.BlockSpec((tm, tn), lambda i, j: (i, j)),
        compiler_params=_params(("parallel", "arbitrary"), vmem + 4 * MIB),
        name="out_proj",
    )(merged, w, x)


def _ffn_down_kernel(h_ref, w_ref, o_ref, *, k_last):
    k = pl.program_id(1)
    n_k = pl.num_programs(1)

    @pl.when(k == 0)
    def _():
        o_ref[...] = jnp.dot(h_ref[...], w_ref[...].astype(BF16), preferred_element_type=F32)

    @pl.when(jnp.logical_and(k > 0, k < n_k - 1))
    def _():
        o_ref[...] += jnp.dot(h_ref[...], w_ref[...].astype(BF16), preferred_element_type=F32)

    @pl.when(k == n_k - 1)
    def _():
        o_ref[...] += jnp.dot(h_ref[:, :k_last], w_ref[:k_last, :].astype(BF16), preferred_element_type=F32)


def _ffn_down(h, w, tm, tk=512):
    m, f = h.shape
    n = w.shape[1]
    n_k = pl.cdiv(f, tk)
    assert n_k >= 2
    vmem = 2 * (tm * tk * 2 + tk * n * 4 + tm * n * 4) + tk * n * 2 + tm * n * 4
    return pl.pallas_call(
        functools.partial(_ffn_down_kernel, k_last=f - (n_k - 1) * tk),
        out_shape=jax.ShapeDtypeStruct((m, n), F32),
        grid=(m // tm, n_k),
        in_specs=[pl.BlockSpec((tm, tk), lambda i, k: (i, k)),
                  pl.BlockSpec((tk, n), lambda i, k: (k, 0))],
        out_specs=pl.BlockSpec((tm, n), lambda i, k: (i, 0)),
        compiler_params=_params(("parallel", "arbitrary"), vmem + 4 * MIB),
        name="ffn_down",
    )(h, w)


def _conv_gate(full0, full1, full2, cw_ref, cb_ref, up):
    c = cb_ref[...] + ((full0 * cw_ref[0:1, :] + full1 * cw_ref[1:2, :]) + full2 * cw_ref[2:3, :])
    return _silu(c) * up


def _zero_after(x):
    r = jnp.max(x, axis=0, keepdims=True).astype(F32)
    bits = pltpu.bitcast(r, jnp.uint32)
    bits = lax.shift_right_logical(lax.shift_right_logical(bits, jnp.uint32(16)), jnp.uint32(16))
    return pltpu.bitcast(bits, F32).astype(BF16)


def _ffn_up_prompt_kernel(xn_ref, wg_ref, wu_ref, cw_ref, cb_ref, h_ref, tail_ref,
                          w_s, u_s, up_s, *, tm, tf, n_blocks, n_steps, blocks_per_seq):
    t = pl.program_id(0)
    i = lax.rem(t, n_blocks)
    n_kt = xn_ref.shape[1] // V7X_MXU_DIM
    rc = tm // n_kt

    def store_result(res):
        u_s[SUBLANES:SUBLANES + tm, :] = res[:, :tf]
        up_s[...] = res[:, tf:]

    def epilogue_rows(r0, nrows):
        h = _conv_gate(u_s[SUBLANES - 2 + r0:SUBLANES - 2 + r0 + nrows, :],
                       u_s[SUBLANES - 1 + r0:SUBLANES - 1 + r0 + nrows, :],
                       u_s[SUBLANES + r0:SUBLANES + r0 + nrows, :], cw_ref, cb_ref,
                       up_s[r0:r0 + nrows, :]).astype(h_ref.dtype)
        h_ref[r0:r0 + nrows, :] = h
        return h

    def finish_epilogue():
        last = u_s[tm:tm + SUBLANES, :]
        tail_ref[0] = last
        return last

    def cast_weights():
        w_s[:, :tf] = wg_ref[...].astype(BF16)
        w_s[:, tf:] = wu_ref[...].astype(BF16)

    @pl.when(t == 0)
    def _():
        cast_weights()
        u_s[0:SUBLANES, :] = jnp.zeros((SUBLANES, u_s.shape[1]), F32)
        store_result(jnp.dot(xn_ref[...], w_s[...], preferred_element_type=F32))

    def steady(first_of_tile):
        if first_of_tile:
            cast_weights()
        pieces = []
        for kt in range(n_kt):
            zero = _zero_after(epilogue_rows(kt * rc, rc))
            w_kt = w_s[kt * V7X_MXU_DIM:(kt + 1) * V7X_MXU_DIM, :]
            pieces.append(w_kt + jnp.concatenate([zero, zero], axis=1))
        last = finish_epilogue()
        res = jnp.dot(xn_ref[...], jnp.concatenate(pieces, axis=0), preferred_element_type=F32)
        u_s[0:SUBLANES, :] = jnp.where(lax.rem(i, blocks_per_seq) == 0, 0.0, last)
        store_result(res)

    @pl.when(jnp.logical_and(t > 0, jnp.logical_and(t < n_steps - 1, i == 0)))
    def _():
        steady(True)

    @pl.when(jnp.logical_and(t < n_steps - 1, i > 0))
    def _():
        steady(False)

    @pl.when(t == n_steps - 1)
    def _():
        epilogue_rows(0, tm)
        finish_epilogue()


def _ffn_up_prompt(xn, w_gate, w_up, conv_w, conv_b, n_seq, tm=1024, tf=256):
    m, d = xn.shape
    f = w_gate.shape[1]
    n_blocks = m // tm
    blocks_per_seq = n_blocks // n_seq
    n_steps = (f // tf) * n_blocks + 1
    assert d % V7X_MXU_DIM == 0 and tm % (d // V7X_MXU_DIM) == 0 and n_blocks >= 2
    vmem = 2 * (tm * d * 2 + 2 * d * tf * 4 + tm * tf * 2) + 4 * d * tf * 2 + (2 * tm + SUBLANES) * tf * 4
    vmem += 4 * tm * tf * 4
    cur = lambda t: jnp.minimum(t, n_steps - 2)
    prev = lambda t: jnp.maximum(t - 1, 0)
    up_tile = lambda t: jnp.minimum(cur(t) + 1, n_steps - 2) // n_blocks
    return pl.pallas_call(
        functools.partial(_ffn_up_prompt_kernel, tm=tm, tf=tf, n_blocks=n_blocks, n_steps=n_steps,
                          blocks_per_seq=blocks_per_seq),
        out_shape=(jax.ShapeDtypeStruct((m, f), BF16),
                   jax.ShapeDtypeStruct((n_seq, SUBLANES, f), F32)),
        grid=(n_steps,),
        in_specs=[pl.BlockSpec((tm, d), lambda t: (cur(t) % n_blocks, 0)),
                  pl.BlockSpec((d, tf), lambda t: (0, cur(t) // n_blocks)),
                  pl.BlockSpec((d, tf), lambda t: (0, up_tile(t))),
                  pl.BlockSpec((CONV_W, tf), lambda t: (0, prev(t) // n_blocks)),
                  pl.BlockSpec((1, tf), lambda t: (0, prev(t) // n_blocks))],
        out_specs=(pl.BlockSpec((tm, tf), lambda t: (prev(t) % n_blocks, prev(t) // n_blocks)),
                   pl.BlockSpec((1, SUBLANES, tf),
                                lambda t: ((prev(t) % n_blocks) // blocks_per_seq, 0, prev(t) // n_blocks))),
        scratch_shapes=[pltpu.VMEM((d, 2 * tf), BF16),
                        pltpu.VMEM((tm + SUBLANES, tf), F32), pltpu.VMEM((tm, tf), F32)],
        compiler_params=_params(("arbitrary",), vmem + 4 * MIB),
        name="ffn_up_prompt",
    )(xn, w_gate, w_up, conv_w, conv_b.reshape(1, f))


def _ffn_up_sample_kernel(xn_ref, wg_ref, wu_ref, cw_ref, cb_ref, buf_ref,
                          h_ref, tail_ref, u_s, up_s, h_s, *, n_seq, t_len):
    xn = xn_ref[...]
    u = jnp.dot(xn, wg_ref[...].astype(BF16), preferred_element_type=F32)
    up = jnp.dot(xn, wu_ref[...].astype(BF16), preferred_element_type=F32)
    for l in range(u_s.shape[0]):
        ls = slice(l * LANES, (l + 1) * LANES)
        u_s[l] = u[:, ls]
        up_s[l] = up[:, ls]
        full = [buf_ref[:, j, ls] for j in range(CONV_W - 1)]
        full += [u_s[l, pl.ds(t, n_seq, stride=t_len), :] for t in range(t_len)]
        for t in range(t_len):
            c = cb_ref[:, ls] + ((full[t] * cw_ref[0:1, ls] + full[t + 1] * cw_ref[1:2, ls])
                                 + full[t + 2] * cw_ref[2:3, ls])
            h_s[l, pl.ds(t, n_seq, stride=t_len), :] = _silu(c) * up_s[l, pl.ds(t, n_seq, stride=t_len), :]
        h_ref[:, ls] = h_s[l].astype(h_ref.dtype)
        for j in range(CONV_W - 1):
            tail_ref[:, j, ls] = full[t_len + j]


def _ffn_up_sample(xn, w_gate, w_up, conv_w, conv_b, conv_buf, n_seq, tf=256):
    m, d = xn.shape
    f = w_gate.shape[1]
    t_len = m // n_seq
    vmem = 2 * (m * d * 2 + 2 * d * tf * 4 + m * tf * 2 + 4 * n_seq * tf * 4) + 2 * d * tf * 2 + 5 * m * tf * 4
    return pl.pallas_call(
        functools.partial(_ffn_up_sample_kernel, n_seq=n_seq, t_len=t_len),
        out_shape=(jax.ShapeDtypeStruct((m, f), BF16),
                   jax.ShapeDtypeStruct((n_seq, CONV_W - 1, f), F32)),
        grid=(f // tf,),
        in_specs=[pl.BlockSpec((m, d), lambda j: (0, 0)),
                  pl.BlockSpec((d, tf), lambda j: (0, j)),
                  pl.BlockSpec((d, tf), lambda j: (0, j)),
                  pl.BlockSpec((CONV_W, tf), lambda j: (0, j)),
                  pl.BlockSpec((1, tf), lambda j: (0, j)),
                  pl.BlockSpec((n_seq, CONV_W - 1, tf), lambda j: (0, 0, j))],
        out_specs=(pl.BlockSpec((m, tf), lambda j: (0, j)),
                   pl.BlockSpec((n_seq, CONV_W - 1, tf), lambda j: (0, 0, j))),
        scratch_shapes=[pltpu.VMEM((tf // LANES, m, LANES), F32)] * 3,
        compiler_params=_params(("parallel",), vmem + 4 * MIB),
        name="ffn_up_sample",
    )(xn, w_gate, w_up, conv_w, conv_b.reshape(1, f), conv_buf)


def _retention_kernel(lg_ref, q_ref, k_ref, v_ref, g_ref, cos_ref, sin_ref, gn_ref, *rest,
                      chunk, n_seg, heads, carry):
    if carry:
        o_ref, sout_ref = rest
    else:
        s0_ref, o_ref, sout_ref = rest
    rows = chunk * n_seg
    shift = int(math.log2(chunk))
    half = RET_DK // 2
    cos = cos_ref[...]
    sin = sin_ref[...]

    def rot(x):
        x1 = x[:, :half]
        x2 = x[:, half:]
        return jnp.concatenate([x1 * cos - x2 * sin, x1 * sin + x2 * cos], axis=-1)

    ri = lax.broadcasted_iota(jnp.int32, (rows, rows), 0)
    ci = lax.broadcasted_iota(jnp.int32, (rows, rows), 1)
    ok = ri >= ci
    if n_seg > 1:
        ok = jnp.logical_and(ok, lax.shift_right_logical(ri, shift) == lax.shift_right_logical(ci, shift))
    diff = jnp.where(ok, (ri - ci).astype(F32), 0.0)
    row_id = lax.broadcasted_iota(jnp.int32, (rows, RET_DK), 0)
    pos = (row_id & (chunk - 1)).astype(F32)
    seg = lax.shift_right_logical(row_id, shift)

    if carry:
        @pl.when(pl.program_id(2) == 0)
        def _():
            sout_ref[...] = jnp.zeros_like(sout_ref)

    for h in range(heads):
        cs = slice(h * RET_DK, (h + 1) * RET_DK)
        log_g = lg_ref[pl.program_id(1) * heads + h]
        dmat = jnp.where(ok, jnp.exp(diff * log_g), 0.0)
        q_dec = jnp.exp((pos + 1.0) * log_g)
        k_dec = jnp.exp((chunk - 1.0 - pos) * log_g)
        s_dec = jnp.exp(jnp.full((1, RET_DV), float(chunk), F32) * log_g)

        qb = rot(q_ref[:, cs]).astype(BF16)
        kr = rot(k_ref[:, cs]) * (RET_DK ** -0.5)
        vb = v_ref[:, cs].astype(BF16)
        scores = lax.dot_general(qb, kr.astype(BF16), _NT, preferred_element_type=F32) * dmat
        o = jnp.dot(scores.astype(BF16), vb, preferred_element_type=F32)
        kd = kr * k_dec
        if carry:
            s = sout_ref[0, h]
            o = o + jnp.dot(qb, s.astype(BF16), preferred_element_type=F32) * q_dec
            sout_ref[0, h] = s * s_dec + lax.dot_general(kd.astype(BF16), vb, _TN, preferred_element_type=F32)
        else:
            for b in range(n_seg):
                mine = seg == b
                s = s0_ref[b, h]
                o_b = jnp.dot(qb, s.astype(BF16), preferred_element_type=F32) * q_dec
                o = o + jnp.where(mine, o_b, 0.0)
                kd_b = jnp.where(mine, kd, 0.0).astype(BF16)
                sout_ref[b, h] = s * s_dec + lax.dot_general(kd_b, vb, _TN, preferred_element_type=F32)

        y = o * lax.rsqrt(jnp.mean(o * o, axis=-1, keepdims=True) + EPS) * gn_ref[:, cs]
        o_ref[:, cs] = (y * _silu(g_ref[:, cs])).astype(o_ref.dtype)


def _retention(proj, log_g, cos, sin, ret_norm_g, state, n_seq, chunk, n_seg, heads):
    m = proj.shape[0]
    rows = chunk * n_seg
    carry = state is None
    width = heads * RET_DK
    cq, ck, cv, cg = (c // width for c in (COL_QR, COL_KR, COL_VR, COL_GR))
    n_hg = RET_HEADS // heads
    if carry:
        n_chunks = m // n_seq // rows
        grid = (n_seq, n_hg, n_chunks)
        row_blk = lambda b, hg, c, lg: b * n_chunks + c
        tab_blk = lambda b, hg, c, lg: (c, 0)
        st_blk = lambda b, hg, c, lg: (b, hg, 0, 0)
        sem = ("parallel", "parallel", "arbitrary")
    else:
        grid = (m // rows, n_hg)
        row_blk = lambda i, hg, lg: i
        tab_blk = lambda i, hg, lg: (0, 0)
        st_blk = lambda i, hg, lg: (i, hg, 0, 0)
        sem = ("parallel", "parallel")
    st_shape = (n_seg, heads, RET_DK, RET_DV)

    def col(c0):
        return pl.BlockSpec((rows, width), lambda *a: (row_blk(*a), c0 + a[1]))

    in_specs = [col(cq), col(ck), col(cv), col(cg),
                pl.BlockSpec((rows, RET_DK // 2), tab_blk),
                pl.BlockSpec((rows, RET_DK // 2), tab_blk),
                pl.BlockSpec((1, width), lambda *a: (0, a[1]))]
    args = [proj, proj, proj, proj, cos, sin, ret_norm_g.reshape(1, RET_V)]
    if not carry:
        in_specs.append(pl.BlockSpec(st_shape, st_blk))
        args.append(state)
    st_bytes = n_seg * heads * RET_DK * RET_DV * 4
    vmem = 2 * (6 * rows * width * 4 + 2 * st_bytes) + 16 * rows * RET_DK * 4 + 4 * RET_DK * RET_DV * 4
    return pl.pallas_call(
        functools.partial(_retention_kernel, chunk=chunk, n_seg=n_seg, heads=heads, carry=carry),
        out_shape=(jax.ShapeDtypeStruct((m, RET_V), BF16),
                   jax.ShapeDtypeStruct((n_seq, RET_HEADS, RET_DK, RET_DV), F32)),
        grid_spec=pltpu.PrefetchScalarGridSpec(
            num_scalar_prefetch=1, grid=grid, in_specs=in_specs,
            out_specs=(pl.BlockSpec((rows, width), lambda *a: (row_blk(*a), a[1])),
                       pl.BlockSpec(st_shape, st_blk))),
        compiler_params=_params(sem, vmem + 8 * MIB),
        name="retention_prompt" if carry else "retention_sample",
    )(log_g, *args)


def _hg_gates(f_raw, i_raw, lb):
    f = lb + (1.0 - lb) * jax.nn.sigmoid(f_raw)
    return jnp.log(f), 1.0 - f, _silu(i_raw)


def _hg_lower_bound(lb_ref):
    logits = lb_ref[...]
    e = jnp.exp(logits - jnp.max(logits, axis=0, keepdims=True))
    return e[0:1, :] / jnp.sum(e, axis=0, keepdims=True)


def _hg_cumsum(log_f, pos, chunk):
    b = log_f
    sh = 1
    while sh < chunk:
        b = b + jnp.where(pos >= sh, pltpu.roll(b, sh, 0), 0.0)
        sh *= 2
    return b


def _hg_intra(q, k, v, b, pos, chunk):
    o = jnp.sum(q * k, axis=-1, keepdims=True) * v
    for d in range(1, chunk):
        valid = pos >= d
        decay = jnp.exp(jnp.where(valid, b - pltpu.roll(b, d, 0), -jnp.inf))
        score = jnp.sum(q * decay * pltpu.roll(k, d, 0), axis=-1, keepdims=True)
        o = o + score * pltpu.roll(v, d, 0)
    return o


def _hg_finish(o, g_raw, gn):
    y = o * lax.rsqrt(jnp.mean(o * o, axis=-1, keepdims=True) + EPS) * gn
    return y * _silu(g_raw)


def _hg_intra_scores(q, k, b2, chunk, lane, row):
    n_blk = chunk // SUBLANES
    qs = [q[i * SUBLANES:(i + 1) * SUBLANES] for i in range(n_blk)]
    bs = [b2[i * SUBLANES:(i + 1) * SUBLANES] for i in range(n_blk)]
    c2 = b2 - jnp.log2(jnp.maximum(k, 0.0))
    p = [jnp.zeros((SUBLANES, HG_DK), F32) for _ in range(n_blk)]
    for s in range(chunk):
        c_s = jnp.broadcast_to(c2[s:s + 1, :], (SUBLANES, HG_DK))
        for i in range(s // SUBLANES, n_blk):
            col = jnp.sum(qs[i] * jnp.exp2(bs[i] - c_s), axis=-1, keepdims=True)
            sel = lane == s
            if i == s // SUBLANES:
                sel = jnp.logical_and(sel, row >= (s - i * SUBLANES))
            p[i] = jnp.where(sel, col, p[i])
    return jnp.concatenate(p, axis=0)


def _hgrn_prompt_kernel(f_ref, q_ref, i_ref, g_ref, lb_ref, gn_ref, o_ref, sout_ref,
                        st_ref, p_s, qd_s, kd_s, v_s, e_s, *, heads, rows, chunk):
    tb = pl.program_id(2)

    @pl.when(tb == 0)
    def _():
        st_ref[...] = jnp.zeros_like(st_ref)

    lb = _hg_lower_bound(lb_ref)
    gn = gn_ref[...]
    pos = lax.broadcasted_iota(jnp.int32, (chunk, HG_DK), 0)
    lane = lax.broadcasted_iota(jnp.int32, (SUBLANES, HG_DK), 1)
    row = lax.broadcasted_iota(jnp.int32, (SUBLANES, HG_DK), 0)

    def front(c):
        rs = pl.ds(pl.multiple_of(c * chunk, chunk), chunk)
        for h in range(heads):
            cs = slice(h * HG_DK, (h + 1) * HG_DK)
            q = q_ref[rs, cs]
            f = lb[:, cs] + (1.0 - lb[:, cs]) * jax.nn.sigmoid(f_ref[rs, cs])
            k = 1.0 - f
            b2 = _hg_cumsum(jnp.log2(f), pos, chunk)
            p_s[h] = _hg_intra_scores(q, k, b2, chunk, lane, row).astype(BF16)
            b_last = b2[chunk - 1:chunk, :]
            qd_s[h] = (q * jnp.exp2(b2)).astype(BF16)
            kd_s[h] = (k * jnp.exp2(b_last - b2)).astype(BF16)
            v_s[h] = _silu(i_ref[rs, cs]).astype(BF16)
            e_s[h] = jnp.broadcast_to(jnp.exp2(b_last), (SUBLANES, HG_DK))

    def back(c):
        rs = pl.ds(pl.multiple_of(c * chunk, chunk), chunk)
        for h in range(heads):
            cs = slice(h * HG_DK, (h + 1) * HG_DK)
            vb = v_s[h]
            st = st_ref[h]
            o = jnp.dot(p_s[h][:, :chunk], vb, preferred_element_type=F32)
            o = o + lax.dot_general(qd_s[h], st.astype(BF16), _NT, preferred_element_type=F32)
            st_ref[h] = st * e_s[h][0:1, :] + lax.dot_general(vb, kd_s[h], _TN, preferred_element_type=F32)
            o_ref[rs, cs] = _hg_finish(o, g_ref[rs, cs], gn[:, cs]).astype(o_ref.dtype)

    def body(c, carry):
        back(c - 1)
        front(c)
        return carry

    n_chunks = rows // chunk
    front(0)
    lax.fori_loop(1, n_chunks, body, 0)
    back(n_chunks - 1)

    @pl.when(tb == pl.num_programs(2) - 1)
    def _():
        for h in range(heads):
            sout_ref[0, h] = st_ref[h].T


def _hgrn_prompt(proj, lb_logits, hg_norm_g, n_seq, chunk, heads=16, rows=512):
    m = proj.shape[0]
    n_tb = m // n_seq // rows
    width = heads * HG_DK
    cf, cq, ci, cg = (c // width for c in (COL_FH, COL_QH, COL_IH, COL_GH))

    def col(c0):
        return pl.BlockSpec((rows, width), lambda b, hg, t: (b * n_tb + t, c0 + hg))

    vmem = 2 * (4 * rows * width * 4 + rows * width * 2 + heads * HG_DK * HG_DV * 4) + heads * HG_DK * HG_DV * 4
    return pl.pallas_call(
        functools.partial(_hgrn_prompt_kernel, heads=heads, rows=rows, chunk=chunk),
        out_shape=(jax.ShapeDtypeStruct((m, HG_V), BF16),
                   jax.ShapeDtypeStruct((n_seq, HG_HEADS, HG_DK, HG_DV), F32)),
        grid=(n_seq, HG_HEADS // heads, n_tb),
        in_specs=[col(cf), col(cq), col(ci), col(cg),
                  pl.BlockSpec((lb_logits.shape[0], width), lambda b, hg, t: (0, hg)),
                  pl.BlockSpec((1, width), lambda b, hg, t: (0, hg))],
        out_specs=(pl.BlockSpec((rows, width), lambda b, hg, t: (b * n_tb + t, hg)),
                   pl.BlockSpec((1, heads, HG_DK, HG_DV), lambda b, hg, t: (b, hg, 0, 0))),
        scratch_shapes=[pltpu.VMEM((heads, HG_DV, HG_DK), F32)]
                       + [pltpu.VMEM((heads, chunk, HG_DK), BF16)] * 4
                       + [pltpu.VMEM((heads, SUBLANES, HG_DK), F32)],
        compiler_params=_params(("parallel", "parallel", "arbitrary"), vmem + 8 * MIB),
        name="hgrn_prompt",
    )(proj, proj, proj, proj, lb_logits, hg_norm_g.reshape(1, HG_V))


def _hgrn_sample_kernel(f_ref, q_ref, i_ref, g_ref, lb_ref, gn_ref, s0_ref, o_ref, sout_ref,
                        *, heads, n_seg, chunk):
    rows = n_seg * chunk
    shift = int(math.log2(chunk))
    lb = _hg_lower_bound(lb_ref)
    gn = gn_ref[...]
    row_id = lax.broadcasted_iota(jnp.int32, (rows, HG_DK), 0)
    pos = row_id & (chunk - 1)
    seg = lax.shift_right_logical(row_id, shift)
    eye = (lax.broadcasted_iota(jnp.int32, (HG_DK, HG_DK), 0)
           == lax.broadcasted_iota(jnp.int32, (HG_DK, HG_DK), 1))
    for h in range(heads):
        cs = slice(h * HG_DK, (h + 1) * HG_DK)
        q = q_ref[:, cs]
        log_f, k, v = _hg_gates(f_ref[:, cs], i_ref[:, cs], lb[:, cs])
        b = _hg_cumsum(log_f, pos, chunk)
        o = _hg_intra(q, k, v, b, pos, chunk)
        b_last = b
        for j in range(1, chunk):
            b_last = jnp.where(pos == chunk - 1 - j, pltpu.roll(b, rows - j, 0), b_last)
        q_dec = (q * jnp.exp(b)).astype(BF16)
        k_dec = k * jnp.exp(b_last - b)
        vb = v.astype(BF16)
        for s in range(n_seg):
            mine = seg == s
            st = s0_ref[s, h]
            o = o + jnp.where(mine, jnp.dot(q_dec, st.astype(BF16), preferred_element_type=F32), 0.0)
            e_row = jnp.exp(b[(s + 1) * chunk - 1:(s + 1) * chunk, :])
            e_col = jnp.sum(jnp.where(eye, e_row, 0.0), axis=-1, keepdims=True)
            k_s = jnp.where(mine, k_dec, 0.0).astype(BF16)
            sout_ref[s, h] = e_col * st + lax.dot_general(k_s, vb, _TN, preferred_element_type=F32)
        o_ref[:, cs] = _hg_finish(o, g_ref[:, cs], gn[:, cs]).astype(o_ref.dtype)


def _hgrn_sample(proj, lb_logits, hg_norm_g, state, chunk, heads=16, n_seg=8):
    m = proj.shape[0]
    n_seq = m // chunk
    rows = n_seg * chunk
    width = heads * HG_DK
    cf, cq, ci, cg = (c // width for c in (COL_FH, COL_QH, COL_IH, COL_GH))

    def col(c0):
        return pl.BlockSpec((rows, width), lambda i, hg: (i, c0 + hg))

    st_spec = pl.BlockSpec((n_seg, heads, HG_DK, HG_DV), lambda i, hg: (i, hg, 0, 0))
    st_bytes = n_seg * heads * HG_DK * HG_DV * 4
    vmem = 2 * (5 * rows * width * 4 + 2 * st_bytes)
    return pl.pallas_call(
        functools.partial(_hgrn_sample_kernel, heads=heads, n_seg=n_seg, chunk=chunk),
        out_shape=(jax.ShapeDtypeStruct((m, HG_V), BF16),
                   jax.ShapeDtypeStruct((n_seq, HG_HEADS, HG_DK, HG_DV), F32)),
        grid=(m // rows, HG_HEADS // heads),
        in_specs=[col(cf), col(cq), col(ci), col(cg),
                  pl.BlockSpec((lb_logits.shape[0], width), lambda i, hg: (0, hg)),
                  pl.BlockSpec((1, width), lambda i, hg: (0, hg)),
                  st_spec],
        out_specs=(pl.BlockSpec((rows, width), lambda i, hg: (i, hg)), st_spec),
        compiler_params=_params(("parallel", "parallel"), vmem + 8 * MIB),
        name="hgrn_sample",
    )(proj, proj, proj, proj, lb_logits, hg_norm_g.reshape(1, HG_V), state)


def _rope_tables(pos):
    half = RET_DK // 2
    inv = ROPE_BASE ** (-jnp.arange(half, dtype=F32) / half)
    ang = pos[:, None] * inv[None, :]
    return jnp.cos(ang), jnp.sin(ang)


def _trunk(x, n_seq, pos, ret_state, hg_state, conv_buf, p, ret_seg, hg_seg):
    m = x.shape[0]
    t_len = m // n_seq
    prompt = ret_state is None
    tm = min(m, 1024)
    ret_chunk = math.gcd(t_len, RET_CHUNK)
    hg_chunk = math.gcd(t_len, HG_CHUNK)
    log_g = jnp.log1p(-jnp.exp2(-5.0 - jnp.arange(RET_HEADS, dtype=F32)))
    cos, sin = _rope_tables(pos)
    if not prompt:
        cos = jnp.tile(cos, (ret_seg, 1))
        sin = jnp.tile(sin, (ret_seg, 1))

    xn = _rmsnorm(x, p["norm_mix_g"], BF16)
    proj = _in_proj(xn, p["w_in"], tm, tn=768 if prompt else 1024)
    o_r, ret_new = _retention(proj, log_g, cos, sin, p["ret_norm_g"], ret_state, n_seq, ret_chunk,
                              1 if prompt else ret_seg, RET_HEADS if prompt else 4)
    if prompt:
        o_h, hg_new = _hgrn_prompt(proj, p["hg_lb_logits"], p["hg_norm_g"], n_seq, hg_chunk)
    else:
        o_h, hg_new = _hgrn_sample(proj, p["hg_lb_logits"], p["hg_norm_g"], hg_state, hg_chunk, n_seg=hg_seg)
    merged = _merge(o_r, o_h, p["w_br_ret"], p["w_br_hg"], proj, tm)
    x1 = _out_proj(merged, p["w_out"], x, tm)

    xn2 = _rmsnorm(x1, p["norm_ffn_g"], BF16)
    if prompt:
        h, tail = _ffn_up_prompt(xn2, p["w_gate"], p["w_up"], p["conv_w"], p["conv_b"], n_seq, tm=tm)
        conv_new = tail[:, SUBLANES - (CONV_W - 1):, :]
    else:
        h, conv_new = _ffn_up_sample(xn2, p["w_gate"], p["w_up"], p["conv_w"], p["conv_b"], conv_buf, n_seq)
    down = _ffn_down(h, p["w_down"], tm)
    y = _rmsnorm(x1, p["final_norm_g"], F32, add=down)
    return y, ret_new, hg_new, conv_new


def kernel(x_prompt, x_sample, state_ret, state_hgrn, state_ffn_conv, norm_mix_g, w_in, ret_norm_g,
           hg_norm_g, hg_lb_logits, w_br_ret, w_br_hg, w_out, norm_ffn_g, w_gate, conv_w, conv_b,
           w_up, w_down, final_norm_g):
    assert norm_mix_g.shape[0] == 1, "single-layer trunk"
    bp, tp, d = x_prompt.shape
    bs, ts, _ = x_sample.shape
    p = dict(norm_mix_g=norm_mix_g[0], w_in=w_in[0], ret_norm_g=ret_norm_g[0], hg_norm_g=hg_norm_g[0],
             hg_lb_logits=hg_lb_logits, w_br_ret=w_br_ret[0], w_br_hg=w_br_hg[0], w_out=w_out[0],
             norm_ffn_g=norm_ffn_g[0], w_gate=w_gate[0], conv_w=conv_w[0], conv_b=conv_b[0],
             w_up=w_up[0], w_down=w_down[0], final_norm_g=final_norm_g)
    pos_p = jnp.arange(tp, dtype=F32)
    pos_s = PAST_LEN + jnp.arange(ts, dtype=F32)

    yp, rp, hp, cp = _trunk(x_prompt.reshape(bp * tp, d), bp, pos_p, None, None, None, p, 1, 1)
    ys, rs, hs, cs = _trunk(x_sample.reshape(bs * ts, d), bs, pos_s, state_ret[0], state_hgrn[0],
                            state_ffn_conv[0], p, 8, 8)
    return (yp.reshape(bp, tp, d), ys.reshape(bs, ts, d), rp[None], hp[None], cp[None],
            rs[None], hs[None], cs[None])
```
